```python
import jax, jax.numpy as jnp
from jax import lax
import numpy as np

D_MODEL = 4096
BATCH = 4
SEQ = 2048
DEPTH = 1
DEC_BATCH = 128
DEC_SEQ = 8
PAST_LEN = 2048
PAGE_SIZE = 128

FOX_HEADS = 16
FOX_HEAD_DIM = 128
FOX_WIDTH = FOX_HEADS * FOX_HEAD_DIM
Q_BLOCK = 128
RET_HEADS = 8
RET_DK = 256
RET_DV = 256
RET_QK_WIDTH = RET_HEADS * RET_DK
RET_WIDTH = RET_HEADS * RET_DV
RET_CHUNK = 128
ROPE_BASE = 10000.0
N_EXPERTS = 32
TOP_K = 4
D_FF = D_MODEL
SWIGLU_LIMIT = 7.0
SWIGLU_ALPHA = 1.702
EPS = 1e-6
N_MOD = 6
IN_SIZES = (FOX_WIDTH, FOX_WIDTH, FOX_WIDTH, FOX_HEADS, RET_QK_WIDTH, RET_QK_WIDTH, RET_WIDTH, RET_WIDTH, D_MODEL, D_MODEL)
N_IN = sum(IN_SIZES)

kernel_name = "fox_retention_gated_moe_decoder_step"


def rms_norm(x, g):
    x32 = x.astype(jnp.float32)
    y = x32 * lax.rsqrt(jnp.mean(x32 * x32, axis=-1, keepdims=True) + EPS)
    return (y * g.astype(jnp.float32)).astype(x.dtype)


def head_group_norm(o, g):
    B, T, H, dv = o.shape
    o32 = o.astype(jnp.float32)
    mu = jnp.mean(o32, axis=-1, keepdims=True)
    var = jnp.mean(jnp.square(o32 - mu), axis=-1, keepdims=True)
    y = ((o32 - mu) * lax.rsqrt(var + EPS)).reshape(B, T, H * dv)
    return (y * g.astype(jnp.float32)).astype(o.dtype)


def rope(x, pos):
    d = x.shape[-1]
    half = d // 2
    inv = ROPE_BASE ** (-jnp.arange(half, dtype=jnp.float32) / half)
    ang = pos.astype(jnp.float32)[:, None] * inv[None, :]
    cos = jnp.cos(ang)[None, :, None, :]
    sin = jnp.sin(ang)[None, :, None, :]
    x32 = x.astype(jnp.float32)
    x1, x2 = x32[..., :half], x32[..., half:]
    return jnp.concatenate([x1 * cos - x2 * sin, x2 * cos + x1 * sin], axis=-1).astype(x.dtype)


def modulation(c, w_ada, b_ada):
    m = (jax.nn.silu(c) @ w_ada + b_ada).reshape(c.shape[0], N_MOD, D_MODEL)
    return [m[:, i][:, None, :] for i in range(N_MOD)]


def in_projection(h, pos, w_in, b_forget, q_norm_g, k_norm_g):
    B, T, _ = h.shape
    z = h @ w_in
    idx = list(np.cumsum(np.array(IN_SIZES))[:-1])
    qf, kf, vf, fl, qr, kr, vr, gr, gate_f, gate_r = jnp.split(z, idx, axis=-1)
    qf = rms_norm(qf.reshape(B, T, FOX_HEADS, FOX_HEAD_DIM), q_norm_g)
    kf = rms_norm(kf.reshape(B, T, FOX_HEADS, FOX_HEAD_DIM), k_norm_g)
    vf = vf.reshape(B, T, FOX_HEADS, FOX_HEAD_DIM)
    logf = jax.nn.log_sigmoid((fl + b_forget).astype(jnp.float32)).astype(h.dtype)
    qr = rope(qr.reshape(B, T, RET_HEADS, RET_DK), pos)
    kr = rope(kr.reshape(B, T, RET_HEADS, RET_DK), pos) * (RET_DK ** -0.5)
    vr = vr.reshape(B, T, RET_HEADS, RET_DV)
    return (qf, kf, vf, logf), (qr, kr, vr), (gr, gate_f, gate_r)


def fox_attend(q, k, v, c_q, c_k, pos_q, pos_k):
    s = jnp.einsum('bqhd,bkhd->bhqk', q, k).astype(jnp.float32) * (FOX_HEAD_DIM ** -0.5)
    bias = jnp.transpose(c_q, (0, 2, 1))[..., :, None] - jnp.transpose(c_k, (0, 2, 1))[..., None, :]
    mask = pos_k[None, :] <= pos_q[:, None]
    s = jnp.where(mask[None, None], s + bias, -jnp.inf)
    p = jax.nn.softmax(s, axis=-1).astype(v.dtype)
    return jnp.einsum('bhqk,bkhd->bqhd', p, v)


def fox_prompt(q, k, v, logf):
    B, S, H, d = q.shape
    c = jnp.cumsum(logf.astype(jnp.float32), axis=1)
    pos = jnp.arange(S, dtype=jnp.int32)
    nb = S // Q_BLOCK
    qb = jnp.transpose(q.reshape(B, nb, Q_BLOCK, H, d), (1, 0, 2, 3, 4))
    cb = jnp.transpose(c.reshape(B, nb, Q_BLOCK, H), (1, 0, 2, 3))
    pb = pos.reshape(nb, Q_BLOCK)
    out = lax.map(lambda a: fox_attend(a[0], k, v, a[1], c, a[2], pos), (qb, cb, pb))
    return jnp.transpose(out, (1, 0, 2, 3, 4)).reshape(B, S, H, d)


def fox_sample(q, k, v, logf, k_past, v_past, logf_past):
    P = k_past.shape[1]
    T = q.shape[1]
    k_all = jnp.concatenate([k_past, k], axis=1)
    v_all = jnp.concatenate([v_past, v], axis=1)
    c = jnp.cumsum(jnp.concatenate([logf_past, logf], axis=1).astype(jnp.float32), axis=1)
    pos_k = jnp.arange(P + T, dtype=jnp.int32)
    pos_q = P + jnp.arange(T, dtype=jnp.int32)
    return fox_attend(q, k_all, v_all, c[:, P:], c, pos_q, pos_k)


def ret_log_gamma():
    return jnp.log1p(-jnp.exp2(-5.0 - jnp.arange(RET_HEADS, dtype=jnp.float32)))


def retention_chunk(state, q, k, v, lg):
    L = q.shape[1]
    n = jnp.arange(L, dtype=jnp.float32)
    diff = n[:, None] - n[None, :]
    decay = jnp.where(diff[None] >= 0, jnp.exp(jnp.maximum(diff, 0.0)[None] * lg[:, None, None]), 0.0)
    scores = jnp.einsum('bnhk,bmhk->bhnm', q, k) * decay.astype(q.dtype)[None]
    o = jnp.einsum('bhnm,bmhv->bnhv', scores, v)
    q_dec = jnp.exp((n + 1.0)[:, None] * lg[None, :]).astype(q.dtype)
    o = o + jnp.einsum('bnhk,bhkv->bnhv', q * q_dec[None, :, :, None], state)
    k_dec = jnp.exp((L - 1.0 - n)[:, None] * lg[None, :]).astype(k.dtype)
    new_state = state * jnp.exp(L * lg).astype(state.dtype)[None, :, None, None] + jnp.einsum('bmhk,bmhv->bhkv', k * k_dec[None, :, :, None], v)
    return new_state, o


def retention_prompt(q, k, v):
    B, S, H, dk = q.shape
    dv = v.shape[-1]
    lg = ret_log_gamma()
    nc = S // RET_CHUNK
    qc = jnp.transpose(q.reshape(B, nc, RET_CHUNK, H, dk), (1, 0, 2, 3, 4))
    kc = jnp.transpose(k.reshape(B, nc, RET_CHUNK, H, dk), (1, 0, 2, 3, 4))
    vc = jnp.transpose(v.reshape(B, nc, RET_CHUNK, H, dv), (1, 0, 2, 3, 4))
    state0 = jnp.zeros((B, H, dk, dv), dtype=q.dtype)
    final, outs = lax.scan(lambda s, a: retention_chunk(s, a[0], a[1], a[2], lg), state0, (qc, kc, vc))
    return final, jnp.transpose(outs, (1, 0, 2, 3, 4)).reshape(B, S, H, dv)


def moe(h, w_router, b_router, w_gate_up, b_gate_up, w_down, b_down):
    shp = h.shape
    x = h.reshape(-1, D_MODEL)
    logits = (x @ w_router + b_router).astype(jnp.float32)
    top_v, top_i = lax.top_k(logits, TOP_K)
    w = jax.nn.softmax(top_v, axis=-1)
    gates = jnp.sum(jax.nn.one_hot(top_i, N_EXPERTS, dtype=jnp.float32) * w[..., None], axis=1).astype(x.dtype)
    y = jnp.zeros_like(x)
    for e in range(N_EXPERTS):
        gu = x @ w_gate_up[e] + b_gate_up[e]
        g = jnp.minimum(gu[:, :D_FF], SWIGLU_LIMIT)
        u = jnp.clip(gu[:, D_FF:], -SWIGLU_LIMIT, SWIGLU_LIMIT)
        act = (u + 1.0) * g * jax.nn.sigmoid(SWIGLU_ALPHA * g)
        y = y + gates[:, e:e + 1] * (act @ w_down[e] + b_down[e])
    return y.reshape(shp)


def decoder_layer(x, c, pos, lw, past):
    (rms1_g, rms2_g, w_ada, b_ada, w_in, b_forget, q_norm_g, k_norm_g, ret_norm_g,
     w_branch_fox, w_branch_ret, w_out, w_router, b_router, w_gate_up, b_gate_up, w_down, b_down) = lw
    B, T, _ = x.shape
    shift1, scale1, gate1, shift2, scale2, gate2 = modulation(c, w_ada, b_ada)
    h = rms_norm(x, rms1_g) * (1.0 + scale1) + shift1
    (qf, kf, vf, logf), (qr, kr, vr), (gr, gate_f, gate_r) = in_projection(h, pos, w_in, b_forget, q_norm_g, k_norm_g)
    if past is None:
        o_f = fox_prompt(qf, kf, vf, logf)
        ret_state, o_r = retention_prompt(qr, kr, vr)
    else:
        k_past, v_past, logf_past, state = past
        o_f = fox_sample(qf, kf, vf, logf, k_past, v_past, logf_past)
        ret_state, o_r = retention_chunk(state, qr, kr, vr, ret_log_gamma())
    br_f = o_f.reshape(B, T, FOX_WIDTH) @ w_branch_fox
    br_r = (head_group_norm(o_r, ret_norm_g) * jax.nn.silu(gr)) @ w_branch_ret
    mixed = (jax.nn.sigmoid(gate_f) * br_f + jax.nn.sigmoid(gate_r) * br_r) @ w_out
    x = x + gate1 * mixed
    h2 = rms_norm(x, rms2_g) * (1.0 + scale2) + shift2
    x = x + gate2 * moe(h2, w_router, b_router, w_gate_up, b_gate_up, w_down, b_down)
    return x, (kf, vf, logf, ret_state)


def setup_inputs(seed: int = 0) -> dict:
    key = jax.random.key(seed)
    ks = jax.random.split(key, 32)
    f32 = jnp.float32

    def nrm(k, shape, scale):
        return jax.random.normal(k, shape, f32) * scale

    n_pages = PAST_LEN // PAGE_SIZE
    n_used = DEC_BATCH * n_pages
    n_pool = n_used + max(n_used // 4, 1)
    page_table = jax.random.permutation(ks[0], n_pool)[:n_used].reshape(DEC_BATCH, n_pages).astype(jnp.int32)
    return {
        "x_prompt": nrm(ks[1], (BATCH, SEQ, D_MODEL), 1.0),
        "x_sample": nrm(ks[2], (DEC_BATCH, DEC_SEQ, D_MODEL), 1.0),
        "c_prompt": nrm(ks[3], (BATCH, D_MODEL), 1.0),
        "c_sample": nrm(ks[4], (DEC_BATCH, D_MODEL), 1.0),
        "cache_k": nrm(ks[5], (DEPTH, n_pool, PAGE_SIZE, FOX_HEADS, FOX_HEAD_DIM), 1.0),
        "cache_v": nrm(ks[6], (DEPTH, n_pool, PAGE_SIZE, FOX_HEADS, FOX_HEAD_DIM), 1.0),
        "cache_logf": jax.nn.log_sigmoid(2.0 + nrm(ks[7], (DEPTH, n_pool, PAGE_SIZE, FOX_HEADS), 0.5)),
        "state_ret": nrm(ks[8], (DEPTH, DEC_BATCH, RET_HEADS, RET_DK, RET_DV), 0.5),
        "page_table": page_table,
        "rms1_g": 1.0 + nrm(ks[9], (DEPTH, D_MODEL), 0.02),
        "rms2_g": 1.0 + nrm(ks[10], (DEPTH, D_MODEL), 0.02),
        "w_ada": nrm(ks[11], (DEPTH, D_MODEL, N_MOD * D_MODEL), D_MODEL ** -0.5),
        "b_ada": nrm(ks[12], (DEPTH, N_MOD * D_MODEL), 0.02),
        "w_in": nrm(ks[13], (DEPTH, D_MODEL, N_IN), D_MODEL ** -0.5),
        "b_forget": 2.0 + nrm(ks[14], (DEPTH, FOX_HEADS), 0.5),
        "q_norm_g": 1.0 + nrm(ks[15], (DEPTH, FOX_HEAD_DIM), 0.02),
        "k_norm_g": 1.0 + nrm(ks[16], (DEPTH, FOX_HEAD_DIM), 0.02),
        "ret_norm_g": 1.0 + nrm(ks[17], (DEPTH, RET_WIDTH), 0.02),
        "w_branch_fox": nrm(ks[18], (DEPTH, FOX_WIDTH, D_MODEL), FOX_WIDTH ** -0.5),
        "w_branch_ret": nrm(ks[19], (DEPTH, RET_WIDTH, D_MODEL), RET_WIDTH ** -0.5),
        "w_out": nrm(ks[20], (DEPTH, D_MODEL, D_MODEL), D_MODEL ** -0.5),
        "w_router": nrm(ks[21], (DEPTH, D_MODEL, N_EXPERTS), D_MODEL ** -0.5),
        "b_router": nrm(ks[22], (DEPTH, N_EXPERTS), 0.01),
        "w_gate_up": nrm(ks[23], (DEPTH, N_EXPERTS, D_MODEL, 2 * D_FF), D_MODEL ** -0.5),
        "b_gate_up": nrm(ks[24], (DEPTH, N_EXPERTS, 2 * D_FF), 0.01),
        "w_down": nrm(ks[25], (DEPTH, N_EXPERTS, D_FF, D_MODEL), D_FF ** -0.5),
        "b_down": nrm(ks[26], (DEPTH, N_EXPERTS, D_MODEL), 0.01),
    }


def reference(x_prompt, x_sample, c_prompt, c_sample, cache_k, cache_v, cache_logf, state_ret, page_table,
              rms1_g, rms2_g, w_ada, b_ada, w_in, b_forget, q_norm_g, k_norm_g, ret_norm_g,
              w_branch_fox, w_branch_ret, w_out, w_router, b_router, w_gate_up, b_gate_up, w_down, b_down):
    dec_b = page_table.shape[0]
    past_len = page_table.shape[1] * cache_k.shape[2]
    pos_p = jnp.arange(x_prompt.shape[1], dtype=jnp.int32)
    pos_s = past_len + jnp.arange(x_sample.shape[1], dtype=jnp.int32)
    yp, ys = x_prompt, x_sample
    kp_l, vp_l, lfp_l, sp_l = [], [], [], []
    ks_l, vs_l, lfs_l, ss_l = [], [], [], []
    for l in range(DEPTH):
        lw = (rms1_g[l], rms2_g[l], w_ada[l], b_ada[l], w_in[l], b_forget[l], q_norm_g[l], k_norm_g[l], ret_norm_g[l],
              w_branch_fox[l], w_branch_ret[l], w_out[l], w_router[l], b_router[l], w_gate_up[l], b_gate_up[l],
              w_down[l], b_down[l])
        yp, (kp, vp, lfp, sp) = decoder_layer(yp, c_prompt, pos_p, lw, None)
        k_past = cache_k[l][page_table].reshape(dec_b, past_len, FOX_HEADS, FOX_HEAD_DIM)
        v_past = cache_v[l][page_table].reshape(dec_b, past_len, FOX_HEADS, FOX_HEAD_DIM)
        lf_past = cache_logf[l][page_table].reshape(dec_b, past_len, FOX_HEADS)
        ys, (kn, vn, lfn, sn) = decoder_layer(ys, c_sample, pos_s, lw, (k_past, v_past, lf_past, state_ret[l]))
        kp_l.append(kp); vp_l.append(vp); lfp_l.append(lfp); sp_l.append(sp)
        ks_l.append(kn); vs_l.append(vn); lfs_l.append(lfn); ss_l.append(sn)
    return (yp, ys,
            jnp.stack(kp_l), jnp.stack(vp_l), jnp.stack(lfp_l), jnp.stack(sp_l),
            jnp.stack(ks_l), jnp.stack(vs_l), jnp.stack(lfs_l), jnp.stack(ss_l))
```

```python
import functools

import jax
import jax.numpy as jnp
import numpy as np
from jax import lax
from jax.experimental import pallas as pl
from jax.experimental.pallas import tpu as pltpu

F32 = jnp.float32
BF16 = jnp.bfloat16
I32 = jnp.int32

D_MODEL = 4096
FOX_HEADS = 16
FOX_HEAD_DIM = 128
FOX_WIDTH = FOX_HEADS * FOX_HEAD_DIM
RET_HEADS = 8
RET_DK = 256
RET_DV = 256
RET_WIDTH = RET_HEADS * RET_DV
RET_CHUNK = 128
ROPE_BASE = 10000.0
N_EXPERTS = 32
TOP_K = 4
D_FF = D_MODEL
SWIGLU_LIMIT = 7.0
SWIGLU_ALPHA = 1.702
EPS = 1e-6
N_MOD = 6
PAGE_SIZE = 128

SUBLANES = 8
LANES = 128
VMEM_LIMIT_BYTES = 56 * 1024 * 1024

TM = 512
TN = 512
CAST_ROWS = 512


def _cparams(*sem):
    return pltpu.CompilerParams(dimension_semantics=sem, vmem_limit_bytes=VMEM_LIMIT_BYTES)


def _cast_rows(src_ref, dst_ref):
    rows = src_ref.shape[0]
    step = min(CAST_ROWS, rows)

    def body(i, _):
        r = pl.multiple_of(i * step, step)
        dst_ref[pl.ds(r, step), :] = src_ref[pl.ds(r, step), :].astype(BF16)
        return 0

    lax.fori_loop(0, rows // step, body, 0)


def _ada_kernel(c_ref, w_ref, b_ref, o_ref, wbf):
    _cast_rows(w_ref, wbf)
    c = c_ref[...]
    a = (c * jax.nn.sigmoid(c)).astype(BF16)
    o_ref[...] = jnp.dot(a, wbf[...], preferred_element_type=F32) + b_ref[...]


def ada_modulation(c_all, w_ada, b_ada):
    rows, d = c_all.shape
    n = w_ada.shape[1]
    return pl.pallas_call(
        _ada_kernel,
        grid=(n // TN,),
        in_specs=[
            pl.BlockSpec((rows, d), lambda j: (0, 0)),
            pl.BlockSpec((d, TN), lambda j: (0, j)),
            pl.BlockSpec((1, TN), lambda j: (0, j)),
        ],
        out_specs=pl.BlockSpec((rows, TN), lambda j: (0, j)),
        out_shape=jax.ShapeDtypeStruct((rows, n), F32),
        scratch_shapes=[pltpu.VMEM((d, TN), BF16)],
        compiler_params=_cparams("arbitrary"),
        name="ada_modulation",
    )(c_all, w_ada, b_ada)


NORM_GROUPS = 16


def _norm_mod_kernel(x_ref, sh_ref, sc_ref, g_ref, o_ref):
    x = x_ref[...]
    ms = jnp.mean(x * x, axis=-1, keepdims=True)
    y = x * lax.rsqrt(ms + EPS) * g_ref[...]
    h = y * (1.0 + sc_ref[...]) + sh_ref[...]
    o_ref[...] = h.reshape(o_ref.shape).astype(o_ref.dtype)


def norm_modulate(x3, shift3, scale3, gain):
    g, r, d = x3.shape
    gb = NORM_GROUPS
    return pl.pallas_call(
        _norm_mod_kernel,
        grid=(g // gb,),
        in_specs=[
            pl.BlockSpec((gb, r, d), lambda i: (i, 0, 0)),
            pl.BlockSpec((gb, 1, d), lambda i: (i, 0, 0)),
            pl.BlockSpec((gb, 1, d), lambda i: (i, 0, 0)),
            pl.BlockSpec((1, 1, d), lambda i: (0, 0, 0)),
        ],
        out_specs=pl.BlockSpec((gb * r, d), lambda i: (i, 0)),
        out_shape=jax.ShapeDtypeStruct((g * r, d), BF16),
        compiler_params=_cparams("arbitrary"),
        name="norm_modulate",
    )(x3, shift3, scale3, gain.reshape(1, 1, d))


def _mm_kernel(*refs, n_a, n_extra, n_out, epilogue):
    a_refs = refs[:n_a]
    w_refs = refs[n_a:2 * n_a]
    extra = refs[2 * n_a:2 * n_a + n_extra]
    outs = refs[2 * n_a + n_extra:2 * n_a + n_extra + n_out]
    wbfs = refs[2 * n_a + n_extra + n_out:]

    @pl.when(pl.program_id(1) == 0)
    def _():
        for w_ref, wbf in zip(w_refs, wbfs):
            _cast_rows(w_ref, wbf)

    accs = [jnp.dot(a[...], wbf[...], preferred_element_type=F32) for a, wbf in zip(a_refs, wbfs)]
    epilogue(accs, extra, outs)


def matmul_ws(a_list, w_list, *, col_offsets, n_blocks, epilogue, out_shapes, out_specs,
              extra=(), extra_specs=(), tn=TN, tm=TM, name):
    m_rows = a_list[0].shape[0]
    in_specs = []
    for a in a_list:
        in_specs.append(pl.BlockSpec((tm, a.shape[1]), lambda n, m: (m, 0)))
    for w, off in zip(w_list, col_offsets):
        in_specs.append(pl.BlockSpec((w.shape[0], tn), functools.partial(lambda n, m, off: (0, n + off), off=off)))
    in_specs += list(extra_specs)
    kernel = functools.partial(_mm_kernel, n_a=len(a_list), n_extra=len(extra), n_out=len(out_shapes),
                               epilogue=epilogue)
    return pl.pallas_call(
        kernel,
        grid=(n_blocks, m_rows // tm),
        in_specs=in_specs,
        out_specs=out_specs,
        out_shape=out_shapes,
        scratch_shapes=[pltpu.VMEM((w.shape[0], tn), BF16) for w in w_list],
        compiler_params=_cparams("arbitrary", "arbitrary"),
        name=name,
    )(*a_list, *w_list, *extra)


def _tile_spec(tm=TM, tn=TN, off=0):
    return pl.BlockSpec((tm, tn), functools.partial(lambda n, m, off: (m, n + off), off=off))


def _ep_plain(accs, extra, outs):
    for o in outs:
        o[...] = accs[0].astype(o.dtype)


def _ep_head_norm(accs, extra, outs, *, mult):
    acc = accs[0]
    g = extra[0][...]
    for j in range(acc.shape[1] // FOX_HEAD_DIM):
        z = acc[:, j * FOX_HEAD_DIM:(j + 1) * FOX_HEAD_DIM]
        ms = jnp.mean(z * z, axis=-1, keepdims=True)
        y = z * lax.rsqrt(ms + EPS) * g
        for o, s in zip(outs, mult):
            o[:, j * FOX_HEAD_DIM:(j + 1) * FOX_HEAD_DIM] = (y if s == 1.0 else y * s).astype(o.dtype)


def _ep_log_forget(accs, extra, outs):
    x = accs[0] + extra[0][...]
    lf = jnp.minimum(x, 0.0) - jnp.log1p(jnp.exp(-jnp.abs(x)))
    lane = lax.broadcasted_iota(I32, lf.shape, 1)
    outs[0][...] = jnp.where(lane < FOX_HEADS, lf, 0.0)


def _ep_rope(accs, extra, outs, *, mult):
    acc = accs[0]
    cos = extra[0][...]
    sin = extra[1][...]
    half = RET_DK // 2
    for j in range(acc.shape[1] // RET_DK):
        x1 = acc[:, j * RET_DK:j * RET_DK + half]
        x2 = acc[:, j * RET_DK + half:(j + 1) * RET_DK]
        y1 = x1 * cos - x2 * sin
        y2 = x2 * cos + x1 * sin
        if mult != 1.0:
            y1 = y1 * mult
            y2 = y2 * mult
        outs[0][:, j * RET_DK:j * RET_DK + half] = y1.astype(outs[0].dtype)
        outs[0][:, j * RET_DK + half:(j + 1) * RET_DK] = y2.astype(outs[0].dtype)


def _ep_branch_merge(accs, extra, outs):
    gf = jax.nn.sigmoid(extra[0][...])
    gr = jax.nn.sigmoid(extra[1][...])
    outs[0][...] = (gf * accs[0] + gr * accs[1]).astype(outs[0].dtype)


IN_FL_START = 3 * FOX_WIDTH
IN_TAIL_START = IN_FL_START + FOX_HEADS
GATE_COLS = RET_WIDTH + 2 * D_MODEL


def in_projection(h, w_in, w_tail, b_forget_pad, q_norm_g, k_norm_g, cos, sin):
    n = h.shape[0]
    sds = jax.ShapeDtypeStruct
    wide = FOX_WIDTH // TN
    g_spec = pl.BlockSpec((1, FOX_HEAD_DIM), lambda j, m: (0, 0))
    rope_specs = [pl.BlockSpec((TM, RET_DK // 2), lambda j, m: (m, 0))] * 2

    (q,) = matmul_ws([h], [w_in], col_offsets=[0], n_blocks=wide,
                     epilogue=functools.partial(_ep_head_norm, mult=(FOX_HEAD_DIM ** -0.5,)),
                     extra=(q_norm_g,), extra_specs=(g_spec,),
                     out_shapes=[sds((n, FOX_WIDTH), BF16)], out_specs=[_tile_spec()], name="in_proj_fox_q")
    k32, kbf = matmul_ws([h], [w_in], col_offsets=[wide], n_blocks=wide,
                         epilogue=functools.partial(_ep_head_norm, mult=(1.0, 1.0)),
                         extra=(k_norm_g,), extra_specs=(g_spec,),
                         out_shapes=[sds((n, FOX_WIDTH), F32), sds((n, FOX_WIDTH), BF16)],
                         out_specs=[_tile_spec(), _tile_spec()], name="in_proj_fox_k")
    v32, vbf = matmul_ws([h], [w_in], col_offsets=[2 * wide], n_blocks=wide, epilogue=_ep_plain,
                         out_shapes=[sds((n, FOX_WIDTH), F32), sds((n, FOX_WIDTH), BF16)],
                         out_specs=[_tile_spec(), _tile_spec()], name="in_proj_fox_v")
    (logf,) = matmul_ws([h], [w_in], col_offsets=[IN_FL_START // LANES], n_blocks=1, tn=LANES,
                        epilogue=_ep_log_forget, extra=(b_forget_pad,),
                        extra_specs=(pl.BlockSpec((1, LANES), lambda j, m: (0, 0)),),
                        out_shapes=[sds((n, LANES), F32)], out_specs=[_tile_spec(tn=LANES)],
                        name="in_proj_log_forget")
    (qr,) = matmul_ws([h], [w_tail], col_offsets=[0], n_blocks=wide,
                      epilogue=functools.partial(_ep_rope, mult=1.0), extra=(cos, sin), extra_specs=rope_specs,
                      out_shapes=[sds((n, RET_WIDTH), BF16)], out_specs=[_tile_spec()], name="in_proj_ret_q")
    (kr,) = matmul_ws([h], [w_tail], col_offsets=[wide], n_blocks=wide,
                      epilogue=functools.partial(_ep_rope, mult=RET_DK ** -0.5), extra=(cos, sin),
                      extra_specs=rope_specs,
                      out_shapes=[sds((n, RET_WIDTH), BF16)], out_specs=[_tile_spec()], name="in_proj_ret_k")
    (vr,) = matmul_ws([h], [w_tail], col_offsets=[2 * wide], n_blocks=wide, epilogue=_ep_plain,
                      out_shapes=[sds((n, RET_WIDTH), BF16)], out_specs=[_tile_spec()], name="in_proj_ret_v")
    (gates,) = matmul_ws([h], [w_tail], col_offsets=[3 * wide], n_blocks=GATE_COLS // TN, epilogue=_ep_plain,
                         out_shapes=[sds((n, GATE_COLS), F32)], out_specs=[_tile_spec()], name="in_proj_gates")
    return q, k32, kbf, v32, vbf, logf, qr, kr, vr, gates


CUMSUM_BLOCK = 256
HIGHEST = lax.Precision.HIGHEST


def _cumsum_kernel(lf_ref, c_ref, ct_ref, carry):
    @pl.when(pl.program_id(1) == 0)
    def _():
        carry[...] = jnp.zeros_like(carry)

    blk = lf_ref.shape[0]
    r = lax.broadcasted_iota(I32, (blk, blk), 0)
    c = lax.broadcasted_iota(I32, (blk, blk), 1)
    lower = (c <= r).astype(F32)
    cs = jnp.dot(lower, lf_ref[...], precision=HIGHEST, preferred_element_type=F32) + carry[0:1, :]
    c_ref[...] = cs
    er = lax.broadcasted_iota(I32, (FOX_HEADS, LANES), 0)
    ec = lax.broadcasted_iota(I32, (FOX_HEADS, LANES), 1)
    eye = (er == ec).astype(F32)
    ct_ref[0] = lax.dot_general(eye, cs, (((1,), (1,)), ((), ())), precision=HIGHEST,
                                preferred_element_type=F32)
    carry[...] = jnp.broadcast_to(cs[blk - 1:blk, :], carry.shape)


def cumsum_log_forget(logf_pad, batch, seq):
    nb = seq // CUMSUM_BLOCK
    return pl.pallas_call(
        _cumsum_kernel,
        grid=(batch, nb),
        in_specs=[pl.BlockSpec((CUMSUM_BLOCK, LANES), lambda b, s: (b * nb + s, 0))],
        out_specs=[
            pl.BlockSpec((CUMSUM_BLOCK, LANES), lambda b, s: (b * nb + s, 0)),
            pl.BlockSpec((1, FOX_HEADS, CUMSUM_BLOCK), lambda b, s: (b, 0, s)),
        ],
        out_shape=[
            jax.ShapeDtypeStruct((batch * seq, LANES), F32),
            jax.ShapeDtypeStruct((batch, FOX_HEADS, seq), F32),
        ],
        scratch_shapes=[pltpu.VMEM((SUBLANES, LANES), F32)],
        compiler_params=_cparams("arbitrary", "arbitrary"),
        name="cumsum_log_forget",
    )(logf_pad)


ATT_BLOCK = 512


def _softmax_update(s, v, m, l, acc):
    m_new = jnp.maximum(m, jnp.max(s, axis=-1, keepdims=True))
    alpha = jnp.exp(m - m_new)
    p = jnp.exp(s - m_new)
    l_new = alpha * l + jnp.sum(p, axis=-1, keepdims=True)
    acc_new = alpha * acc + jnp.dot(p.astype(BF16), v, preferred_element_type=F32)
    return m_new, l_new, acc_new


def _fox_prompt_kernel(q_ref, k_ref, v_ref, c_ref, ct_ref, o_ref):
    h = pl.program_id(1)
    qi = pl.program_id(2)
    t = ATT_BLOCK
    q = q_ref[...]
    lane = lax.broadcasted_iota(I32, (t, LANES), 1)
    cq = jnp.sum(jnp.where(lane == h, c_ref[...], 0.0), axis=1, keepdims=True)
    sub = lax.broadcasted_iota(I32, (FOX_HEADS, t), 0)

    def scores(j):
        start = pl.multiple_of(j * t, t)
        k = k_ref[pl.ds(start, t), :]
        s = lax.dot_general(q, k, (((1,), (1,)), ((), ())), preferred_element_type=F32)
        ck = jnp.sum(jnp.where(sub == h, ct_ref[0, :, pl.ds(start, t)], 0.0), axis=0, keepdims=True)
        return s + (cq - ck), v_ref[pl.ds(start, t), :]

    def body(j, carry):
        s, v = scores(j)
        return _softmax_update(s, v, *carry)

    init = (jnp.full((t, 1), -jnp.inf, F32), jnp.zeros((t, 1), F32), jnp.zeros((t, FOX_HEAD_DIM), F32))
    carry = lax.fori_loop(0, qi, body, init)
    s, v = scores(qi)
    row = lax.broadcasted_iota(I32, (t, t), 0)
    col = lax.broadcasted_iota(I32, (t, t), 1)
    s = jnp.where(col <= row, s, -jnp.inf)
    _, l, acc = _softmax_update(s, v, *carry)
    o_ref[...] = (acc / l).astype(o_ref.dtype)


def fox_prompt_attention(q, k, v, c, ct, batch, seq):
    nq = seq // ATT_BLOCK
    hd = FOX_HEAD_DIM
    return pl.pallas_call(
        _fox_prompt_kernel,
        grid=(batch, FOX_HEADS, nq),
        in_specs=[
            pl.BlockSpec((ATT_BLOCK, hd), lambda b, h, i: (b * nq + i, h)),
            pl.BlockSpec((seq, hd), lambda b, h, i: (b, h)),
            pl.BlockSpec((seq, hd), lambda b, h, i: (b, h)),
            pl.BlockSpec((ATT_BLOCK, LANES), lambda b, h, i: (b * nq + i, 0)),
            pl.BlockSpec((1, FOX_HEADS, seq), lambda b, h, i: (b, 0, 0)),
        ],
        out_specs=pl.BlockSpec((ATT_BLOCK, hd), lambda b, h, i: (b * nq + i, h)),
        out_shape=jax.ShapeDtypeStruct((batch * seq, FOX_WIDTH), BF16),
        compiler_params=_cparams("arbitrary", "arbitrary", "arbitrary"),
        name="fox_prompt_attention",
    )(q, k, v, c, ct)


def _expand_heads(x):
    rows = [jnp.broadcast_to(x[h:h + 1, :], (SUBLANES, x.shape[1])) for h in range(FOX_HEADS)]
    return jnp.concatenate(rows, axis=0)


def _fox_sample_kernel(pt_ref, q_ref, kc_ref, vc_ref, lft_ref, kn_ref, vn_ref, lfnt_ref, o_ref,
                       qh, m_sc, l_sc, acc_sc, carry_sc, cq_sc, cnk_sc, *, n_pages):
    jj = pl.program_id(1)
    rows = FOX_HEADS * SUBLANES
    hd = FOX_HEAD_DIM
    row = lax.broadcasted_iota(I32, (rows, LANES), 0)
    lane = lax.broadcasted_iota(I32, (rows, LANES), 1)
    row_head = row // SUBLANES
    row_t = row % SUBLANES

    @pl.when(jj == 0)
    def _():
        q = q_ref[...]
        for h in range(FOX_HEADS):
            qt = jnp.concatenate([q[:, h * hd:(h + 1) * hd]] * FOX_HEADS, axis=0)
            qh[h] = jnp.where(row_head == h, qt, 0.0).astype(BF16)
        m_sc[...] = jnp.full(m_sc.shape, -jnp.inf, F32)
        l_sc[...] = jnp.zeros(l_sc.shape, F32)
        acc_sc[...] = jnp.zeros(acc_sc.shape, F32)
        carry_sc[...] = jnp.zeros(carry_sc.shape, F32)
        ui = lax.broadcasted_iota(I32, (LANES, LANES), 0)
        uj = lax.broadcasted_iota(I32, (LANES, LANES), 1)
        cnt = jnp.dot(lfnt_ref[0], (ui <= uj).astype(F32), precision=HIGHEST, preferred_element_type=F32)
        cnk = _expand_heads(cnt)
        cnk_sc[...] = cnk
        cq_sc[...] = jnp.sum(jnp.where(lane == row_t, cnk, 0.0), axis=1, keepdims=True)

    def attend(get_k, get_v, bias, mask):
        s = jnp.zeros((rows, LANES), F32)
        for h in range(FOX_HEADS):
            s = s + lax.dot_general(qh[h], get_k(h).astype(BF16), (((1,), (1,)), ((), ())),
                                    preferred_element_type=F32)
        s = s + bias
        if mask is not None:
            s = jnp.where(mask, s, -jnp.inf)
        m = m_sc[...]
        m_new = jnp.maximum(m, jnp.max(s, axis=-1, keepdims=True))
        alpha = jnp.exp(m - m_new)
        p = jnp.exp(s - m_new)
        l_sc[...] = alpha * l_sc[...] + jnp.sum(p, axis=-1, keepdims=True)
        m_sc[...] = m_new
        pv = jnp.zeros((rows, hd), F32)
        for h in range(FOX_HEADS):
            ph = jnp.where(row_head == h, p, 0.0).astype(BF16)
            pv = pv + jnp.dot(ph, get_v(h).astype(BF16), preferred_element_type=F32)
        acc_sc[...] = alpha * acc_sc[...] + pv

    @pl.when(jj < n_pages)
    def _():
        lft = jnp.concatenate([lft_ref[0], jnp.zeros((LANES - FOX_HEADS, LANES), F32)], axis=0)
        ui = lax.broadcasted_iota(I32, (LANES, LANES), 0)
        uj = lax.broadcasted_iota(I32, (LANES, LANES), 1)
        dt = jnp.dot(lft, (ui > uj).astype(F32), precision=HIGHEST, preferred_element_type=F32) + carry_sc[...]
        carry_sc[...] = carry_sc[...] + jnp.sum(lft, axis=1, keepdims=True)
        bias = _expand_heads(dt[:FOX_HEADS, :]) + cq_sc[...]
        attend(lambda h: kc_ref[0, :, h, :], lambda h: vc_ref[0, :, h, :], bias, None)

    @pl.when(jj == n_pages)
    def _():
        kn = kn_ref[...]
        vn = vn_ref[...]
        zpad = jnp.zeros((LANES - SUBLANES, hd), F32)
        bias = cq_sc[...] - cnk_sc[...]
        mask = (lane < SUBLANES) & (lane <= row_t)
        attend(lambda h: jnp.concatenate([kn[:, h * hd:(h + 1) * hd], zpad], axis=0),
               lambda h: jnp.concatenate([vn[:, h * hd:(h + 1) * hd], zpad], axis=0), bias, mask)
        out = acc_sc[...] / l_sc[...]
        for h in range(FOX_HEADS):
            o_ref[:, h * hd:(h + 1) * hd] = out[h * SUBLANES:(h + 1) * SUBLANES, :]


def fox_sample_attention(page_table, q, cache_k, cache_v, cache_logf_t, k_new, v_new, logf_new_t):
    bd, n_pages = page_table.shape
    hd = FOX_HEAD_DIM
    rows = FOX_HEADS * SUBLANES

    def page_idx(b, j, pt):
        return pt[b * n_pages + (n_pages - 1 - jnp.minimum(j, n_pages - 1))]

    grid_spec = pltpu.PrefetchScalarGridSpec(
        num_scalar_prefetch=1,
        grid=(bd, n_pages + 1),
        in_specs=[
            pl.BlockSpec((SUBLANES, FOX_WIDTH), lambda b, j, pt: (b, 0)),
            pl.BlockSpec((1, PAGE_SIZE, FOX_HEADS, hd), lambda b, j, pt: (page_idx(b, j, pt), 0, 0, 0)),
            pl.BlockSpec((1, PAGE_SIZE, FOX_HEADS, hd), lambda b, j, pt: (page_idx(b, j, pt), 0, 0, 0)),
            pl.BlockSpec((1, FOX_HEADS, PAGE_SIZE), lambda b, j, pt: (page_idx(b, j, pt), 0, 0)),
            pl.BlockSpec((SUBLANES, FOX_WIDTH), lambda b, j, pt: (b, 0)),
            pl.BlockSpec((SUBLANES, FOX_WIDTH), lambda b, j, pt: (b, 0)),
            pl.BlockSpec((1, FOX_HEADS, LANES), lambda b, j, pt: (b, 0, 0)),
        ],
        out_specs=pl.BlockSpec((SUBLANES, FOX_WIDTH), lambda b, j, pt: (b, 0)),
        scratch_shapes=[
            pltpu.VMEM((FOX_HEADS, rows, hd), BF16),
            pltpu.VMEM((rows, 1), F32),
            pltpu.VMEM((rows, 1), F32),
            pltpu.VMEM((rows, hd), F32),
            pltpu.VMEM((LANES, 1), F32),
            pltpu.VMEM((rows, 1), F32),
            pltpu.VMEM((rows, LANES), F32),
        ],
    )
    return pl.pallas_call(
        functools.partial(_fox_sample_kernel, n_pages=n_pages),
        grid_spec=grid_spec,
        out_shape=jax.ShapeDtypeStruct((bd * SUBLANES, FOX_WIDTH), F32),
        compiler_params=_cparams("arbitrary", "arbitrary"),
        name="fox_sample_attention",
    )(page_table.reshape(-1), q, cache_k, cache_v, cache_logf_t, k_new, v_new, logf_new_t)


def _group_norm_gate(o, gain, gate):
    mu = jnp.mean(o, axis=-1, keepdims=True)
    d = o - mu
    var = jnp.mean(d * d, axis=-1, keepdims=True)
    y = d * lax.rsqrt(var + EPS) * gain
    return y * (gate * jax.nn.sigmoid(gate))


def _ret_prompt_kernel(q_ref, k_ref, v_ref, gr_ref, g_ref, dmat_ref, qdec_ref, kdec_ref, sdec_ref,
                       o_ref, st_ref, state):
    state[...] = jnp.zeros_like(state)
    n_chunks = q_ref.shape[0] // RET_CHUNK
    dmat = dmat_ref[0]
    qdec = qdec_ref[0]
    kdec = kdec_ref[0]
    sdec = sdec_ref[0]
    gain = g_ref[...]

    def body(c, _):
        r = pl.multiple_of(c * RET_CHUNK, RET_CHUNK)
        qc = q_ref[pl.ds(r, RET_CHUNK), :]
        kc = k_ref[pl.ds(r, RET_CHUNK), :]
        vc = v_ref[pl.ds(r, RET_CHUNK), :]
        s = lax.dot_general(qc, kc, (((1,), (1,)), ((), ())), preferred_element_type=F32) * dmat
        st = state[...]
        o = jnp.dot(s.astype(BF16), vc, preferred_element_type=F32)
        o = o + jnp.dot((qc.astype(F32) * qdec).astype(BF16), st.astype(BF16), preferred_element_type=F32)
        kd = (kc.astype(F32) * kdec).astype(BF16)
        state[...] = st * sdec + lax.dot_general(kd, vc, (((0,), (0,)), ((), ())), preferred_element_type=F32)
        o_ref[pl.ds(r, RET_CHUNK), :] = _group_norm_gate(o, gain, gr_ref[pl.ds(r, RET_CHUNK), :]).astype(o_ref.dtype)
        return 0

    lax.fori_loop(0, n_chunks, body, 0)
    st_ref[0, 0] = state[...]


def retention_prompt(qr, kr, vr, gates, ret_norm_g, consts, batch, seq):
    dmat, qdec, kdec, sdec = consts
    blk = lambda b, h: (b, h)
    hconst = lambda b, h: (h, 0, 0)
    return pl.pallas_call(
        _ret_prompt_kernel,
        grid=(batch, RET_HEADS),
        in_specs=[
            pl.BlockSpec((seq, RET_DK), blk),
            pl.BlockSpec((seq, RET_DK), blk),
            pl.BlockSpec((seq, RET_DV), blk),
            pl.BlockSpec((seq, RET_DV), blk),
            pl.BlockSpec((1, RET_DV), lambda b, h: (0, h)),
            pl.BlockSpec((1, RET_CHUNK, RET_CHUNK), hconst),
            pl.BlockSpec((1, RET_CHUNK, RET_DK), hconst),
            pl.BlockSpec((1, RET_CHUNK, RET_DK), hconst),
            pl.BlockSpec((1, 1, RET_DV), hconst),
        ],
        out_specs=[
            pl.BlockSpec((seq, RET_DV), blk),
            pl.BlockSpec((1, 1, RET_DK, RET_DV), lambda b, h: (b, h, 0, 0)),
        ],
        out_shape=[
            jax.ShapeDtypeStruct((batch * seq, RET_WIDTH), BF16),
            jax.ShapeDtypeStruct((batch, RET_HEADS, RET_DK, RET_DV), F32),
        ],
        scratch_shapes=[pltpu.VMEM((RET_DK, RET_DV), F32)],
        compiler_params=_cparams("arbitrary", "arbitrary"),
        name="retention_prompt",
    )(qr, kr, vr, gates, ret_norm_g, dmat, qdec, kdec, sdec)


RET_SAMPLE_Q_ROWS = 16


def _ret_sample_kernel(q_ref, k_ref, v_ref, gr_ref, g_ref, st_in, dmat_ref, qdec_ref, kdec_ref, sdec_ref,
                       o_ref, st_out):
    t = q_ref.shape[0]
    qpad = jnp.zeros((RET_SAMPLE_Q_ROWS - t, RET_DK), F32)
    kpad = jnp.zeros((RET_CHUNK - t, RET_DK), F32)
    for h in range(RET_HEADS):
        cols = slice(h * RET_DK, (h + 1) * RET_DK)
        q = jnp.concatenate([q_ref[:, cols], qpad], axis=0)
        k = jnp.concatenate([k_ref[:, cols], kpad], axis=0)
        v = jnp.concatenate([v_ref[:, cols], kpad], axis=0).astype(BF16)
        st = st_in[0, h]
        s = lax.dot_general(q.astype(BF16), k.astype(BF16), (((1,), (1,)), ((), ())),
                            preferred_element_type=F32) * dmat_ref[h]
        o = jnp.dot(s.astype(BF16), v, preferred_element_type=F32)
        o = o + jnp.dot((q * qdec_ref[h]).astype(BF16), st.astype(BF16), preferred_element_type=F32)
        kd = (k * kdec_ref[h]).astype(BF16)
        st_out[0, h] = st * sdec_ref[h] + lax.dot_general(kd, v, (((0,), (0,)), ((), ())),
                                                         preferred_element_type=F32)
        o_ref[:, cols] = _group_norm_gate(o[:t, :], g_ref[:, cols], gr_ref[:, cols])


def retention_sample(qr, kr, vr, gr, ret_norm_g, state, consts):
    dmat, qdec, kdec, sdec = consts
    bd = state.shape[0]
    t = qr.shape[0] // bd
    row = lambda b: (b, 0)
    const3 = lambda b: (0, 0, 0)
    return pl.pallas_call(
        _ret_sample_kernel,
        grid=(bd,),
        in_specs=[
            pl.BlockSpec((t, RET_WIDTH), row),
            pl.BlockSpec((t, RET_WIDTH), row),
            pl.BlockSpec((t, RET_WIDTH), row),
            pl.BlockSpec((t, RET_WIDTH), row),
            pl.BlockSpec((1, RET_WIDTH), lambda b: (0, 0)),
            pl.BlockSpec((1, RET_HEADS, RET_DK, RET_DV), lambda b: (b, 0, 0, 0)),
            pl.BlockSpec(dmat.shape, const3),
            pl.BlockSpec(qdec.shape, const3),
            pl.BlockSpec(kdec.shape, const3),
            pl.BlockSpec(sdec.shape, const3),
        ],
        out_specs=[
            pl.BlockSpec((t, RET_WIDTH), row),
            pl.BlockSpec((1, RET_HEADS, RET_DK, RET_DV), lambda b: (b, 0, 0, 0)),
        ],
        out_shape=[
            jax.ShapeDtypeStruct((bd * t, RET_WIDTH), F32),
            jax.ShapeDtypeStruct(state.shape, F32),
        ],
        compiler_params=_cparams("arbitrary"),
        name="retention_sample",
    )(qr, kr, vr, gr, ret_norm_g, state, dmat, qdec, kdec, sdec)


def _retention_consts(chunk, q_rows, k_rows):
    lg = jnp.log1p(-jnp.exp2(-5.0 - jnp.arange(RET_HEADS, dtype=F32)))
    n = jnp.arange(chunk, dtype=F32)
    diff = n[:, None] - n[None, :]
    decay = jnp.where(diff[None] >= 0, jnp.exp(jnp.maximum(diff, 0.0)[None] * lg[:, None, None]), 0.0)
    dmat = jnp.zeros((RET_HEADS, q_rows, k_rows), F32).at[:, :chunk, :chunk].set(decay)
    q_dec = jnp.exp((n + 1.0)[None, :] * lg[:, None])
    k_dec = jnp.exp((chunk - 1.0 - n)[None, :] * lg[:, None])
    qdec = jnp.zeros((RET_HEADS, q_rows, RET_DK), F32).at[:, :chunk, :].set(
        jnp.broadcast_to(q_dec[:, :, None], (RET_HEADS, chunk, RET_DK)))
    kdec = jnp.zeros((RET_HEADS, k_rows, RET_DK), F32).at[:, :chunk, :].set(
        jnp.broadcast_to(k_dec[:, :, None], (RET_HEADS, chunk, RET_DK)))
    sdec = jnp.broadcast_to(jnp.exp(chunk * lg)[:, None, None], (RET_HEADS, 1, RET_DV))
    return dmat, qdec, kdec, sdec


ROUTER_LANES = LANES


def _resid_router_kernel(x_ref, mx_ref, g1_ref, sh_ref, sc_ref, g_ref, wr_ref, br_ref,
                         x1_ref, h2_ref, tw_ref, ti_ref):
    x1 = x_ref[...] + g1_ref[...] * mx_ref[...]
    x1_ref[...] = x1
    ms = jnp.mean(x1 * x1, axis=-1, keepdims=True)
    y = x1 * lax.rsqrt(ms + EPS) * g_ref[...]
    h3 = y * (1.0 + sc_ref[...]) + sh_ref[...]
    rows = h3.shape[0] * h3.shape[1]
    h2 = h3.reshape(rows, h3.shape[2])

    half = h2.shape[1] // 2
    bits = pltpu.bitcast(h2.astype(BF16).astype(F32), jnp.uint32)
    h2_ref[...] = (bits[:, :half] >> 16) | (bits[:, half:] & jnp.uint32(0xFFFF0000))

    logits = jnp.dot(h2, wr_ref[...], precision=HIGHEST, preferred_element_type=F32) + br_ref[...]
    lane = lax.broadcasted_iota(I32, logits.shape, 1)
    logits = jnp.where(lane < N_EXPERTS, logits, -jnp.inf)
    vals, idxs = [], []
    for _ in range(TOP_K):
        m = jnp.max(logits, axis=-1, keepdims=True)
        idx = jnp.min(jnp.where(logits == m, lane, ROUTER_LANES), axis=-1, keepdims=True)
        vals.append(m)
        idxs.append(idx)
        logits = jnp.where(lane == idx, -jnp.inf, logits)
    exps = [jnp.exp(v - vals[0]) for v in vals]
    denom = exps[0] + exps[1] + exps[2] + exps[3]
    tw = jnp.zeros(lane.shape, F32)
    ti = jnp.zeros(lane.shape, I32)
    for k in range(TOP_K):
        tw = jnp.where(lane == k, exps[k] / denom, tw)
        ti = jnp.where(lane == k, idxs[k], ti)
    tw_ref[...] = tw
    ti_ref[...] = ti


def resid_norm_router(x3, mixed3, gate1, shift2, scale2, gain, w_router_pad, b_router_pad):
    g, r, d = x3.shape
    gb = NORM_GROUPS
    rows = gb * r
    n = g * r
    grp = lambda i: (i, 0, 0)
    tok = lambda i: (i, 0)
    return pl.pallas_call(
        _resid_router_kernel,
        grid=(g // gb,),
        in_specs=[
            pl.BlockSpec((gb, r, d), grp),
            pl.BlockSpec((gb, r, d), grp),
            pl.BlockSpec((gb, 1, d), grp),
            pl.BlockSpec((gb, 1, d), grp),
            pl.BlockSpec((gb, 1, d), grp),
            pl.BlockSpec((1, 1, d), lambda i: (0, 0, 0)),
            pl.BlockSpec((d, ROUTER_LANES), lambda i: (0, 0)),
            pl.BlockSpec((1, ROUTER_LANES), lambda i: (0, 0)),
        ],
        out_specs=[
            pl.BlockSpec((gb, r, d), grp),
            pl.BlockSpec((rows, d // 2), tok),
            pl.BlockSpec((rows, ROUTER_LANES), tok),
            pl.BlockSpec((rows, ROUTER_LANES), tok),
        ],
        out_shape=[
            jax.ShapeDtypeStruct((g, r, d), F32),
            jax.ShapeDtypeStruct((n, d // 2), jnp.uint32),
            jax.ShapeDtypeStruct((n, ROUTER_LANES), F32),
            jax.ShapeDtypeStruct((n, ROUTER_LANES), I32),
        ],
        compiler_params=_cparams("arbitrary"),
        name="resid_norm_router",
    )(x3, mixed3, gate1, shift2, scale2, gain.reshape(1, 1, d), w_router_pad, b_router_pad)


MOE_SUB = 256
MOE_SUBS_PER_TILE = 5
MOE_TILE = MOE_SUB * MOE_SUBS_PER_TILE
GATHER_ROWS = 128
FF_TN = 256
DOWN_TN = 512
COMBINE_TOKENS = 64


def _dispatch_kernel(tok_ref, cstart_ref, nv_ref, src_hbm, o_ref, sem):
    i = pl.program_id(0)
    nv = nv_ref[i]
    base = cstart_ref[i]

    @pl.when(nv < GATHER_ROWS)
    def _():
        o_ref[...] = jnp.zeros(o_ref.shape, o_ref.dtype)

    def row_copy(r, t):
        return pltpu.make_async_copy(src_hbm.at[pl.ds(t, 1), :], o_ref.at[pl.ds(r, 1), :], sem)

    def issue(r, _):
        row_copy(r, tok_ref[base + r]).start()
        return 0

    def wait(r, _):
        row_copy(r, 0).wait()
        return 0

    lax.fori_loop(0, nv, issue, 0)
    lax.fori_loop(0, nv, wait, 0)


def moe_dispatch(src_tok, blk_start, n_valid, h2_packed, m_pad):
    width = h2_packed.shape[1]
    grid_spec = pltpu.PrefetchScalarGridSpec(
        num_scalar_prefetch=3,
        grid=(m_pad // GATHER_ROWS,),
        in_specs=[pl.BlockSpec(memory_space=pl.ANY)],
        out_specs=pl.BlockSpec((GATHER_ROWS, width), lambda i, tok, cs, nv: (i, 0)),
        scratch_shapes=[pltpu.SemaphoreType.DMA(())],
    )
    return pl.pallas_call(
        _dispatch_kernel,
        grid_spec=grid_spec,
        out_shape=jax.ShapeDtypeStruct((m_pad, width), h2_packed.dtype),
        compiler_params=_cparams("arbitrary"),
        name="moe_dispatch",
    )(src_tok, blk_start, n_valid, h2_packed)


def _unpack_bf16_pairs(words):
    lo = pltpu.bitcast(words << 16, F32).astype(BF16)
    hi = pltpu.bitcast(words & jnp.uint32(0xFFFF0000), F32).astype(BF16)
    return lo, hi


def _moe_up_kernel(e_ref, blk_ref, ns_ref, x_ref, wg_ref, wu_ref, bg_ref, bu_ref, o_ref,
                   xlo, xhi, wg_bf, wu_bf):
    s = pl.program_id(0)
    nsub = ns_ref[s]
    half = xlo.shape[1]

    @pl.when((pl.program_id(1) == 0) & (nsub > 0))
    def _():
        for i in range(MOE_SUBS_PER_TILE):
            @pl.when(i < nsub)
            def _():
                rows = slice(i * MOE_SUB, (i + 1) * MOE_SUB)
                lo, hi = _unpack_bf16_pairs(x_ref[rows, :])
                xlo[rows, :] = lo
                xhi[rows, :] = hi

    @pl.when(nsub > 0)
    def _():
        _cast_rows(wg_ref, wg_bf)
        _cast_rows(wu_ref, wu_bf)
        for i in range(MOE_SUBS_PER_TILE):
            rows = slice(i * MOE_SUB, (i + 1) * MOE_SUB)

            @pl.when(i < nsub)
            def _():
                lo = xlo[rows, :]
                hi = xhi[rows, :]
                g = (jnp.dot(lo, wg_bf[:half, :], preferred_element_type=F32)
                     + jnp.dot(hi, wg_bf[half:, :], preferred_element_type=F32) + bg_ref[...])
                u = (jnp.dot(lo, wu_bf[:half, :], preferred_element_type=F32)
                     + jnp.dot(hi, wu_bf[half:, :], preferred_element_type=F32) + bu_ref[...])
                g = jnp.minimum(g, SWIGLU_LIMIT)
                u = jnp.clip(u, -SWIGLU_LIMIT, SWIGLU_LIMIT)
                o_ref[rows, :] = ((u + 1.0) * g * jax.nn.sigmoid(SWIGLU_ALPHA * g)).astype(o_ref.dtype)

            @pl.when(i >= nsub)
            def _():
                o_ref[rows, :] = jnp.zeros((MOE_SUB, o_ref.shape[1]), o_ref.dtype)


def _frozen_col(n, ns, s, last):
    return jnp.where(ns[s] > 0, n, last)


def moe_up(st_expert, st_block, st_nsub, xs, w_gate_up, b_gate_up):
    m_pad, half = xs.shape
    d = 2 * half
    n_steps = st_expert.shape[0]
    nb = D_FF // FF_TN
    last = nb - 1
    grid_spec = pltpu.PrefetchScalarGridSpec(
        num_scalar_prefetch=3,
        grid=(n_steps, nb),
        in_specs=[
            pl.BlockSpec((MOE_TILE, half), lambda s, n, e, blk, ns: (blk[s], 0)),
            pl.BlockSpec((None, d, FF_TN), lambda s, n, e, blk, ns: (e[s], 0, _frozen_col(n, ns, s, last))),
            pl.BlockSpec((None, d, FF_TN), lambda s, n, e, blk, ns: (e[s], 0, nb + _frozen_col(n, ns, s, last))),
            pl.BlockSpec((None, 1, FF_TN), lambda s, n, e, blk, ns: (e[s], 0, _frozen_col(n, ns, s, last))),
            pl.BlockSpec((None, 1, FF_TN), lambda s, n, e, blk, ns: (e[s], 0, nb + _frozen_col(n, ns, s, last))),
        ],
        out_specs=pl.BlockSpec((MOE_TILE, FF_TN), lambda s, n, e, blk, ns: (blk[s], _frozen_col(n, ns, s, last))),
        scratch_shapes=[
            pltpu.VMEM((MOE_TILE, half), BF16),
            pltpu.VMEM((MOE_TILE, half), BF16),
            pltpu.VMEM((d, FF_TN), BF16),
            pltpu.VMEM((d, FF_TN), BF16),
        ],
    )
    return pl.pallas_call(
        _moe_up_kernel,
        grid_spec=grid_spec,
        out_shape=jax.ShapeDtypeStruct((m_pad, D_FF), BF16),
        compiler_params=_cparams("arbitrary", "arbitrary"),
        name="moe_up",
    )(st_expert, st_block, st_nsub, xs, w_gate_up, w_gate_up, b_gate_up, b_gate_up)


def _moe_down_kernel(e_ref, blk_ref, ns_ref, x_ref, w_ref, b_ref, o_ref, w_bf):
    nsub = ns_ref[pl.program_id(0)]

    @pl.when(nsub > 0)
    def _():
        _cast_rows(w_ref, w_bf)
        for i in range(MOE_SUBS_PER_TILE):
            rows = slice(i * MOE_SUB, (i + 1) * MOE_SUB)

            @pl.when(i < nsub)
            def _():
                o_ref[rows, :] = jnp.dot(x_ref[rows, :], w_bf[...], preferred_element_type=F32) + b_ref[...]

            @pl.when(i >= nsub)
            def _():
                o_ref[rows, :] = jnp.zeros((MOE_SUB, o_ref.shape[1]), o_ref.dtype)


def moe_down(st_expert, st_block, st_nsub, act, w_down, b_down):
    m_pad, dff = act.shape
    d = w_down.shape[2]
    n_steps = st_expert.shape[0]
    nb = d // DOWN_TN
    last = nb - 1
    grid_spec = pltpu.PrefetchScalarGridSpec(
        num_scalar_prefetch=3,
        grid=(n_steps, nb),
        in_specs=[
            pl.BlockSpec((MOE_TILE, dff), lambda s, n, e, blk, ns: (blk[s], 0)),
            pl.BlockSpec((None, dff, DOWN_TN), lambda s, n, e, blk, ns: (e[s], 0, _frozen_col(n, ns, s, last))),
            pl.BlockSpec((None, 1, DOWN_TN), lambda s, n, e, blk, ns: (e[s], 0, _frozen_col(n, ns, s, last))),
        ],
        out_specs=pl.BlockSpec((MOE_TILE, DOWN_TN), lambda s, n, e, blk, ns: (blk[s], _frozen_col(n, ns, s, last))),
        scratch_shapes=[pltpu.VMEM((dff, DOWN_TN), BF16)],
    )
    return pl.pallas_call(
        _moe_down_kernel,
        grid_spec=grid_spec,
        out_shape=jax.ShapeDtypeStruct((m_pad, d), F32),
        compiler_params=_cparams("arbitrary", "arbitrary"),
        name="moe_down",
    )(st_expert, st_block, st_nsub, act, w_down, b_down)


def _combine_kernel(pos_ref, ys_hbm, x1_ref, g2_ref, tw_ref, o_ref, buf, sems):
    i = pl.program_id(0)
    n_steps = pl.num_programs(0)
    t = COMBINE_TOKENS

    def row_copy(slot, k, r, p):
        return pltpu.make_async_copy(ys_hbm.at[pl.ds(p, 1), :], buf.at[slot, k, pl.ds(r, 1), :], sems.at[slot])

    def start(step, slot):
        def issue(r, _):
            for k in range(TOP_K):
                row_copy(slot, k, r, pos_ref[(step * t + r) * TOP_K + k]).start()
            return 0

        lax.fori_loop(0, t, issue, 0)

    @pl.when(i == 0)
    def _():
        start(0, 0)

    @pl.when(i + 1 < n_steps)
    def _():
        start(i + 1, (i + 1) % 2)

    slot = i % 2

    def wait(r, _):
        for k in range(TOP_K):
            row_copy(slot, k, r, 0).wait()
        return 0

    lax.fori_loop(0, t, wait, 0)
    tw = tw_ref[...]
    y = tw[:, 0:1] * buf[slot, 0]
    for k in range(1, TOP_K):
        y = y + tw[:, k:k + 1] * buf[slot, k]
    d = y.shape[1]
    y3 = y.reshape(t // SUBLANES, SUBLANES, d)
    o_ref[...] = x1_ref[...] + g2_ref[...] * y3


def moe_combine(pos, ys, x1_3, gate2, top_w):
    g, r, d = x1_3.shape
    gb = COMBINE_TOKENS // r
    grid_spec = pltpu.PrefetchScalarGridSpec(
        num_scalar_prefetch=1,
        grid=(g // gb,),
        in_specs=[
            pl.BlockSpec(memory_space=pl.ANY),
            pl.BlockSpec((gb, r, d), lambda i, pos: (i, 0, 0)),
            pl.BlockSpec((gb, 1, d), lambda i, pos: (i, 0, 0)),
            pl.BlockSpec((COMBINE_TOKENS, ROUTER_LANES), lambda i, pos: (i, 0)),
        ],
        out_specs=pl.BlockSpec((gb, r, d), lambda i, pos: (i, 0, 0)),
        scratch_shapes=[
            pltpu.VMEM((2, TOP_K, COMBINE_TOKENS, d), F32),
            pltpu.SemaphoreType.DMA((2,)),
        ],
    )
    return pl.pallas_call(
        _combine_kernel,
        grid_spec=grid_spec,
        out_shape=jax.ShapeDtypeStruct((g, r, d), F32),
        compiler_params=_cparams("arbitrary"),
        name="moe_combine",
    )(pos, ys, x1_3, gate2, top_w)


def _routing_tables(top_i, n_tiles):
    n = top_i.shape[0]
    m = n * TOP_K
    flat_e = top_i.reshape(m)
    order = jnp.argsort(flat_e, stable=True).astype(I32)
    e_sorted = flat_e[order]
    counts = jnp.sum(jax.nn.one_hot(flat_e, N_EXPERTS, dtype=I32), axis=0)
    starts = jnp.cumsum(counts) - counts
    tiles_per_e = (counts + MOE_TILE - 1) // MOE_TILE
    tile_end = jnp.cumsum(tiles_per_e)
    tile_start = tile_end - tiles_per_e
    total_tiles = tile_end[-1]
    rank = jnp.arange(m, dtype=I32) - starts[e_sorted]
    dest = tile_start[e_sorted] * MOE_TILE + rank
    pos = jnp.zeros((m,), I32).at[order].set(dest)
    src_tok = order // TOP_K
    s = jnp.arange(n_tiles, dtype=I32)
    s_eff = jnp.minimum(s, total_tiles - 1)
    st_expert = jnp.searchsorted(tile_end, s_eff, side="right").astype(I32)
    within = (s_eff - tile_start[st_expert]) * MOE_TILE
    rows_valid = jnp.where(s < total_tiles, jnp.clip(counts[st_expert] - within, 0, MOE_TILE), 0)
    st_nsub = ((rows_valid + MOE_SUB - 1) // MOE_SUB).astype(I32)
    per_tile = MOE_TILE // GATHER_ROWS
    off = (jnp.arange(per_tile, dtype=I32) * GATHER_ROWS)[None, :]
    blk_start = (starts[st_expert] + within)[:, None] + off
    n_valid = jnp.clip(rows_valid[:, None] - off, 0, GATHER_ROWS)
    blk_start = jnp.where(n_valid > 0, blk_start, 0).reshape(-1).astype(I32)
    n_valid = n_valid.reshape(-1).astype(I32)
    return pos, src_tok.astype(I32), blk_start, n_valid, st_expert, s_eff.astype(I32), st_nsub


def _rope_tables(pos):
    half = RET_DK // 2
    inv = ROPE_BASE ** (-jnp.arange(half, dtype=F32) / half)
    ang = pos.astype(F32)[:, None] * inv[None, :]
    return jnp.cos(ang), jnp.sin(ang)


def _pad_rows(x, rows):
    return jnp.concatenate([x, jnp.zeros((rows - x.shape[0],) + x.shape[1:], x.dtype)], axis=0)


def kernel(x_prompt, x_sample, c_prompt, c_sample, cache_k, cache_v, cache_logf, state_ret, page_table, rms1_g, rms2_g, w_ada, b_ada, w_in, b_forget, q_norm_g, k_norm_g, ret_norm_g, w_branch_fox, w_branch_ret, w_out, w_router, b_router, w_gate_up, b_gate_up, w_down, b_down):
    batch, seq, d = x_prompt.shape
    bd, dec_t, _ = x_sample.shape
    depth = w_in.shape[0]
    assert depth == 1 and d == D_MODEL and dec_t == SUBLANES
    n_p = batch * seq
    n_s = bd * dec_t
    n = n_p + n_s
    groups = n // SUBLANES
    past_len = page_table.shape[1] * cache_k.shape[2]

    c_all = _pad_rows(jnp.concatenate([c_prompt, c_sample], axis=0), 136)
    mod = ada_modulation(c_all, w_ada[0], b_ada[0].reshape(1, -1))
    group_seq = jnp.concatenate([jnp.repeat(jnp.arange(batch, dtype=I32), seq // SUBLANES),
                                 batch + jnp.arange(bd, dtype=I32)])
    modg = mod[group_seq].reshape(groups, N_MOD, d)
    shift1, scale1, gate1, shift2, scale2, gate2 = [modg[:, i:i + 1, :] for i in range(N_MOD)]

    x_all = jnp.concatenate([x_prompt.reshape(n_p, d), x_sample.reshape(n_s, d)], axis=0)
    x3 = x_all.reshape(groups, SUBLANES, d)
    h = norm_modulate(x3, shift1, scale1, rms1_g[0])

    pos = jnp.concatenate([jnp.tile(jnp.arange(seq, dtype=I32), batch),
                           jnp.tile(past_len + jnp.arange(dec_t, dtype=I32), bd)])
    cos, sin = _rope_tables(pos)
    w_in0 = w_in[0]
    w_tail = w_in0[:, IN_TAIL_START:]
    b_forget_pad = jnp.zeros((1, LANES), F32).at[0, :FOX_HEADS].set(b_forget[0])
    q, k32, kbf, v32, vbf, logf, qr, kr, vr, gates = in_projection(
        h, w_in0, w_tail, b_forget_pad, q_norm_g[0].reshape(1, -1), k_norm_g[0].reshape(1, -1), cos, sin)

    c_cum, c_cum_t = cumsum_log_forget(logf, batch, seq)
    o_fox_p = fox_prompt_attention(q, kbf, vbf, c_cum, c_cum_t, batch, seq)
    cache_logf_t = jnp.swapaxes(cache_logf[0], 1, 2)
    logf_s = logf[n_p:, :FOX_HEADS].reshape(bd, dec_t, FOX_HEADS)
    logf_new_t = jnp.zeros((bd, FOX_HEADS, LANES), F32).at[:, :, :dec_t].set(jnp.swapaxes(logf_s, 1, 2))
    o_fox_s = fox_sample_attention(page_table, q[n_p:].astype(F32), cache_k[0], cache_v[0], cache_logf_t,
                                   k32[n_p:], v32[n_p:], logf_new_t)
    o_fox = jnp.concatenate([o_fox_p, o_fox_s.astype(BF16)], axis=0)

    g_ret = ret_norm_g[0].reshape(1, -1)
    o_ret_p, state_p = retention_prompt(qr, kr, vr, gates, g_ret,
                                        _retention_consts(RET_CHUNK, RET_CHUNK, RET_CHUNK), batch, seq)
    o_ret_s, state_s = retention_sample(qr[n_p:].astype(F32), kr[n_p:].astype(F32), vr[n_p:].astype(F32),
                                        gates[n_p:, :RET_WIDTH], g_ret, state_ret[0],
                                        _retention_consts(dec_t, RET_SAMPLE_Q_ROWS, RET_CHUNK))
    o_ret = jnp.concatenate([o_ret_p, o_ret_s.astype(BF16)], axis=0)

    gate_blocks = RET_WIDTH // TN
    (mixed_in,) = matmul_ws(
        [o_fox, o_ret], [w_branch_fox[0], w_branch_ret[0]], col_offsets=[0, 0], n_blocks=d // TN,
        epilogue=_ep_branch_merge, extra=(gates, gates),
        extra_specs=(_tile_spec(off=gate_blocks), _tile_spec(off=gate_blocks + d // TN)),
        out_shapes=[jax.ShapeDtypeStruct((n, d), BF16)], out_specs=[_tile_spec()], name="branch_merge")
    (mixed,) = matmul_ws([mixed_in], [w_out[0]], col_offsets=[0], n_blocks=d // TN, epilogue=_ep_plain,
                         out_shapes=[jax.ShapeDtypeStruct((n, d), F32)], out_specs=[_tile_spec()], name="out_proj")

    w_router_pad = jnp.zeros((d, ROUTER_LANES), F32).at[:, :N_EXPERTS].set(w_router[0])
    b_router_pad = jnp.zeros((1, ROUTER_LANES), F32).at[0, :N_EXPERTS].set(b_router[0])
    x1, h2_packed, top_w, top_i = resid_norm_router(
        x3, mixed.reshape(groups, SUBLANES, d), gate1, shift2, scale2, rms2_g[0], w_router_pad, b_router_pad)

    n_tiles = N_EXPERTS + (n * TOP_K) // MOE_TILE
    pos_rows, src_tok, blk_start, n_valid, st_expert, st_block, st_nsub = _routing_tables(top_i[:, :TOP_K], n_tiles)
    xs = moe_dispatch(src_tok, blk_start, n_valid, h2_packed, n_tiles * MOE_TILE)
    act = moe_up(st_expert, st_block, st_nsub, xs, w_gate_up[0], b_gate_up[0].reshape(N_EXPERTS, 1, -1))
    ys = moe_down(st_expert, st_block, st_nsub, act, w_down[0], b_down[0].reshape(N_EXPERTS, 1, -1))
    y = moe_combine(pos_rows, ys, x1, gate2, top_w).reshape(n, d)

    y_prompt = y[:n_p].reshape(batch, seq, d)
    y_sample = y[n_p:].reshape(bd, dec_t, d)
    kv_p = (depth, batch, seq, FOX_HEADS, FOX_HEAD_DIM)
    kv_s = (depth, bd, dec_t, FOX_HEADS, FOX_HEAD_DIM)
    logf16 = logf[:, :FOX_HEADS]
    return (y_prompt, y_sample,
            k32[:n_p].reshape(kv_p), v32[:n_p].reshape(kv_p), logf16[:n_p].reshape(depth, batch, seq, FOX_HEADS),
            state_p[None],
            k32[n_p:].reshape(kv_s), v32[n_p:].reshape(kv_s), logf16[n_p:].reshape(depth, bd, dec_t, FOX_HEADS),
            state_s[None])
```

```python
import functools

import jax
import jax.numpy as jnp
import numpy as np
from jax import lax
from jax.experimental import pallas as pl
from jax.experimental.pallas import tpu as pltpu

F32 = jnp.float32
BF16 = jnp.bfloat16
I32 = jnp.int32

D_MODEL = 4096
FOX_HEADS = 16
FOX_HEAD_DIM = 128
FOX_WIDTH = FOX_HEADS * FOX_HEAD_DIM
RET_HEADS = 8
RET_DK = 256
RET_DV = 256
RET_WIDTH = RET_HEADS * RET_DV
RET_CHUNK = 128
ROPE_BASE = 10000.0
N_EXPERTS = 32
TOP_K = 4
D_FF = D_MODEL
SWIGLU_LIMIT = 7.0
SWIGLU_ALPHA = 1.702
EPS = 1e-6
N_MOD = 6
PAGE_SIZE = 128

SUBLANES = 8
LANES = 128
VMEM_LIMIT_BYTES = 56 * 1024 * 1024

TM = 512
TN = 512
CAST_ROWS = 512


def _cparams(*sem):
    return pltpu.CompilerParams(dimension_semantics=sem, vmem_limit_bytes=VMEM_LIMIT_BYTES)


def _cast_rows(src_ref, dst_ref):
    rows = src_ref.shape[0]
    step = min(CAST_ROWS, rows)

    def body(i, _):
        r = pl.multiple_of(i * step, step)
        dst_ref[pl.ds(r, step), :] = src_ref[pl.ds(r, step), :].astype(BF16)
        return 0

    lax.fori_loop(0, rows // step, body, 0)


def _ada_kernel(c_ref, w_ref, b_ref, o_ref, wbf):
    _cast_rows(w_ref, wbf)
    c = c_ref[...]
    a = (c * jax.nn.sigmoid(c)).astype(BF16)
    o_ref[...] = jnp.dot(a, wbf[...], preferred_element_type=F32) + b_ref[...]


def ada_modulation(c_all, w_ada, b_ada):
    rows, d = c_all.shape
    n = w_ada.shape[1]
    return pl.pallas_call(
        _ada_kernel,
        grid=(n // TN,),
        in_specs=[
            pl.BlockSpec((rows, d), lambda j: (0, 0)),
            pl.BlockSpec((d, TN), lambda j: (0, j)),
            pl.BlockSpec((1, TN), lambda j: (0, j)),
        ],
        out_specs=pl.BlockSpec((rows, TN), lambda j: (0, j)),
        out_shape=jax.ShapeDtypeStruct((rows, n), F32),
        scratch_shapes=[pltpu.VMEM((d, TN), BF16)],
        compiler_params=_cparams("arbitrary"),
        name="ada_modulation",
    )(c_all, w_ada, b_ada)


NORM_GROUPS = 16
MOD_REPEAT = 16


def _mod_spec(chunk, gb, prompt_groups, groups_per_seq, d):
    assert MOD_REPEAT % gb == 0 and groups_per_seq % gb == 0 and prompt_groups % gb == 0
    n_prompt_rows = (prompt_groups // groups_per_seq) * MOD_REPEAT

    def index(i, *_):
        g0 = i * gb
        prompt_blk = (g0 // groups_per_seq) * (MOD_REPEAT // gb)
        sample_blk = (n_prompt_rows + g0 - prompt_groups) // gb
        return (chunk, jnp.where(g0 < prompt_groups, prompt_blk, sample_blk), 0, 0)

    return pl.BlockSpec((None, gb, 1, d), index)


def _norm_mod_kernel(x_ref, sh_ref, sc_ref, g_ref, o_ref):
    x = x_ref[...]
    ms = jnp.mean(x * x, axis=-1, keepdims=True)
    y = x * lax.rsqrt(ms + EPS) * g_ref[...]
    h = y * (1.0 + sc_ref[...]) + sh_ref[...]
    o_ref[...] = h.reshape(o_ref.shape).astype(o_ref.dtype)


def norm_modulate(x3, mod4, shift_chunk, scale_chunk, gain, prompt_groups, groups_per_seq):
    g, r, d = x3.shape
    gb = NORM_GROUPS
    mod_spec = functools.partial(_mod_spec, gb=gb, prompt_groups=prompt_groups, groups_per_seq=groups_per_seq, d=d)
    return pl.pallas_call(
        _norm_mod_kernel,
        grid=(g // gb,),
        in_specs=[
            pl.BlockSpec((gb, r, d), lambda i: (i, 0, 0)),
            mod_spec(shift_chunk),
            mod_spec(scale_chunk),
            pl.BlockSpec((1, 1, d), lambda i: (0, 0, 0)),
        ],
        out_specs=pl.BlockSpec((gb * r, d), lambda i: (i, 0)),
        out_shape=jax.ShapeDtypeStruct((g * r, d), BF16),
        compiler_params=_cparams("arbitrary"),
        name="norm_modulate",
    )(x3, mod4, mod4, gain.reshape(1, 1, d))


def _cast_rows_transposed(src_ref, dst_ref):
    k = src_ref.shape[1]
    for c in range(k // CAST_ROWS):
        rows = slice(c * CAST_ROWS, (c + 1) * CAST_ROWS)
        dst_ref[rows, :] = src_ref[:, rows].T.astype(BF16)


def _mm_kernel(*refs, n_a, n_extra, n_out, epilogue, transposed):
    a_refs = refs[:n_a]
    w_refs = refs[n_a:2 * n_a]
    extra = refs[2 * n_a:2 * n_a + n_extra]
    outs = refs[2 * n_a + n_extra:2 * n_a + n_extra + n_out]
    wbfs = refs[2 * n_a + n_extra + n_out:]

    @pl.when(pl.program_id(1) == 0)
    def _():
        for w_ref, wbf in zip(w_refs, wbfs):
            (_cast_rows_transposed if transposed else _cast_rows)(w_ref, wbf)

    accs = [jnp.dot(a[...], wbf[...], preferred_element_type=F32) for a, wbf in zip(a_refs, wbfs)]
    epilogue(accs, extra, outs)


def matmul_ws(a_list, w_list, *, col_offsets, n_blocks, epilogue, out_shapes, out_specs,
              extra=(), extra_specs=(), tn=TN, tm=TM, transposed=False, name):
    m_rows = a_list[0].shape[0]
    in_specs = []
    for a in a_list:
        in_specs.append(pl.BlockSpec((tm, a.shape[1]), lambda n, m: (m, 0)))
    for w, off in zip(w_list, col_offsets):
        if transposed:
            assert off % SUBLANES == 0 and tn % SUBLANES == 0
            in_specs.append(pl.BlockSpec(
                (pl.Element(tn), pl.Element(w.shape[1])),
                functools.partial(lambda n, m, off: (pl.multiple_of(off + n * tn, SUBLANES), 0), off=off)))
        else:
            in_specs.append(pl.BlockSpec((w.shape[0], tn),
                                         functools.partial(lambda n, m, off: (0, n + off), off=off)))
    in_specs += list(extra_specs)
    kernel = functools.partial(_mm_kernel, n_a=len(a_list), n_extra=len(extra), n_out=len(out_shapes),
                               epilogue=epilogue, transposed=transposed)
    k_dims = [w.shape[1] if transposed else w.shape[0] for w in w_list]
    return pl.pallas_call(
        kernel,
        grid=(n_blocks, m_rows // tm),
        in_specs=in_specs,
        out_specs=out_specs,
        out_shape=out_shapes,
        scratch_shapes=[pltpu.VMEM((k, tn), BF16) for k in k_dims],
        compiler_params=_cparams("arbitrary", "arbitrary"),
        name=name,
    )(*a_list, *w_list, *extra)


def _tile_spec(tm=TM, tn=TN, off=0):
    return pl.BlockSpec((tm, tn), functools.partial(lambda n, m, off: (m, n + off), off=off))


def _ep_plain(accs, extra, outs):
    for o in outs:
        o[...] = accs[0].astype(o.dtype)


def _ep_head_norm(accs, extra, outs, *, mult):
    acc = accs[0]
    g = extra[0][...]
    for j in range(acc.shape[1] // FOX_HEAD_DIM):
        z = acc[:, j * FOX_HEAD_DIM:(j + 1) * FOX_HEAD_DIM]
        ms = jnp.mean(z * z, axis=-1, keepdims=True)
        y = z * lax.rsqrt(ms + EPS) * g
        for o, s in zip(outs, mult):
            o[:, j * FOX_HEAD_DIM:(j + 1) * FOX_HEAD_DIM] = (y if s == 1.0 else y * s).astype(o.dtype)


def _ep_log_forget(accs, extra, outs):
    x = accs[0] + extra[0][...]
    lf = jnp.minimum(x, 0.0) - jnp.log1p(jnp.exp(-jnp.abs(x)))
    lane = lax.broadcasted_iota(I32, lf.shape, 1)
    outs[0][...] = jnp.where(lane < FOX_HEADS, lf, 0.0)


def _ep_rope(accs, extra, outs, *, mult):
    acc = accs[0]
    cos = extra[0][...]
    sin = extra[1][...]
    half = RET_DK // 2
    for j in range(acc.shape[1] // RET_DK):
        x1 = acc[:, j * RET_DK:j * RET_DK + half]
        x2 = acc[:, j * RET_DK + half:(j + 1) * RET_DK]
        y1 = x1 * cos - x2 * sin
        y2 = x2 * cos + x1 * sin
        if mult != 1.0:
            y1 = y1 * mult
            y2 = y2 * mult
        outs[0][:, j * RET_DK:j * RET_DK + half] = y1.astype(outs[0].dtype)
        outs[0][:, j * RET_DK + half:(j + 1) * RET_DK] = y2.astype(outs[0].dtype)


def _ep_branch_merge(accs, extra, outs):
    gf = jax.nn.sigmoid(extra[0][...])
    gr = jax.nn.sigmoid(extra[1][...])
    outs[0][...] = (gf * accs[0] + gr * accs[1]).astype(outs[0].dtype)


IN_FL_START = 3 * FOX_WIDTH
IN_TAIL_START = IN_FL_START + FOX_HEADS
GATE_COLS = RET_WIDTH + 2 * D_MODEL


def in_projection(h, w_in_t, b_forget_pad, q_norm_g, k_norm_g, cos, sin):
    n = h.shape[0]
    sds = jax.ShapeDtypeStruct
    wide = FOX_WIDTH // TN
    g_spec = pl.BlockSpec((1, FOX_HEAD_DIM), lambda j, m: (0, 0))
    rope_specs = [pl.BlockSpec((TM, RET_DK // 2), lambda j, m: (m, 0))] * 2
    mm = functools.partial(matmul_ws, [h], [w_in_t], transposed=True)

    (q,) = mm(col_offsets=[0], n_blocks=wide,
              epilogue=functools.partial(_ep_head_norm, mult=(FOX_HEAD_DIM ** -0.5,)),
              extra=(q_norm_g,), extra_specs=(g_spec,),
              out_shapes=[sds((n, FOX_WIDTH), BF16)], out_specs=[_tile_spec()], name="in_proj_fox_q")
    k32, kbf = mm(col_offsets=[FOX_WIDTH], n_blocks=wide,
                  epilogue=functools.partial(_ep_head_norm, mult=(1.0, 1.0)),
                  extra=(k_norm_g,), extra_specs=(g_spec,),
                  out_shapes=[sds((n, FOX_WIDTH), F32), sds((n, FOX_WIDTH), BF16)],
                  out_specs=[_tile_spec(), _tile_spec()], name="in_proj_fox_k")
    v32, vbf = mm(col_offsets=[2 * FOX_WIDTH], n_blocks=wide, epilogue=_ep_plain,
                  out_shapes=[sds((n, FOX_WIDTH), F32), sds((n, FOX_WIDTH), BF16)],
                  out_specs=[_tile_spec(), _tile_spec()], name="in_proj_fox_v")
    (logf,) = mm(col_offsets=[IN_FL_START], n_blocks=1, tn=LANES,
                 epilogue=_ep_log_forget, extra=(b_forget_pad,),
                 extra_specs=(pl.BlockSpec((1, LANES), lambda j, m: (0, 0)),),
                 out_shapes=[sds((n, LANES), F32)], out_specs=[_tile_spec(tn=LANES)],
                 name="in_proj_log_forget")
    (qr,) = mm(col_offsets=[IN_TAIL_START], n_blocks=wide,
               epilogue=functools.partial(_ep_rope, mult=1.0), extra=(cos, sin), extra_specs=rope_specs,
               out_shapes=[sds((n, RET_WIDTH), BF16)], out_specs=[_tile_spec()], name="in_proj_ret_q")
    (kr,) = mm(col_offsets=[IN_TAIL_START + RET_WIDTH], n_blocks=wide,
               epilogue=functools.partial(_ep_rope, mult=RET_DK ** -0.5), extra=(cos, sin),
               extra_specs=rope_specs,
               out_shapes=[sds((n, RET_WIDTH), BF16)], out_specs=[_tile_spec()], name="in_proj_ret_k")
    (vr,) = mm(col_offsets=[IN_TAIL_START + 2 * RET_WIDTH], n_blocks=wide, epilogue=_ep_plain,
               out_shapes=[sds((n, RET_WIDTH), BF16)], out_specs=[_tile_spec()], name="in_proj_ret_v")
    (gates,) = mm(col_offsets=[IN_TAIL_START + 3 * RET_WIDTH], n_blocks=GATE_COLS // TN, epilogue=_ep_plain,
                  out_shapes=[sds((n, GATE_COLS), F32)], out_specs=[_tile_spec()], name="in_proj_gates")
    return q, k32, kbf, v32, vbf, logf, qr, kr, vr, gates


CUMSUM_BLOCK = 256
HIGHEST = lax.Precision.HIGHEST


def _cumsum_kernel(lf_ref, c_ref, ct_ref, carry):
    @pl.when(pl.program_id(1) == 0)
    def _():
        carry[...] = jnp.zeros_like(carry)

    blk = lf_ref.shape[0]
    r = lax.broadcasted_iota(I32, (blk, blk), 0)
    c = lax.broadcasted_iota(I32, (blk, blk), 1)
    lower = (c <= r).astype(F32)
    cs = jnp.dot(lower, lf_ref[...], precision=HIGHEST, preferred_element_type=F32) + carry[0:1, :]
    c_ref[...] = cs
    er = lax.broadcasted_iota(I32, (FOX_HEADS, LANES), 0)
    ec = lax.broadcasted_iota(I32, (FOX_HEADS, LANES), 1)
    eye = (er == ec).astype(F32)
    ct_ref[0] = lax.dot_general(eye, cs, (((1,), (1,)), ((), ())), precision=HIGHEST,
                                preferred_element_type=F32)
    carry[...] = jnp.broadcast_to(cs[blk - 1:blk, :], carry.shape)


def cumsum_log_forget(logf_pad, batch, seq):
    nb = seq // CUMSUM_BLOCK
    return pl.pallas_call(
        _cumsum_kernel,
        grid=(batch, nb),
        in_specs=[pl.BlockSpec((CUMSUM_BLOCK, LANES), lambda b, s: (b * nb + s, 0))],
        out_specs=[
            pl.BlockSpec((CUMSUM_BLOCK, LANES), lambda b, s: (b * nb + s, 0)),
            pl.BlockSpec((1, FOX_HEADS, CUMSUM_BLOCK), lambda b, s: (b, 0, s)),
        ],
        out_shape=[
            jax.ShapeDtypeStruct((batch * seq, LANES), F32),
            jax.ShapeDtypeStruct((batch, FOX_HEADS, seq), F32),
        ],
        scratch_shapes=[pltpu.VMEM((SUBLANES, LANES), F32)],
        compiler_params=_cparams("arbitrary", "arbitrary"),
        name="cumsum_log_forget",
    )(logf_pad)


ATT_BLOCK = 512


def _softmax_update(s, v, m, l, acc):
    m_new = jnp.maximum(m, jnp.max(s, axis=-1, keepdims=True))
    alpha = jnp.exp(m - m_new)
    p = jnp.exp(s - m_new)
    l_new = alpha * l + jnp.sum(p, axis=-1, keepdims=True)
    acc_new = alpha * acc + jnp.dot(p.astype(BF16), v, preferred_element_type=F32)
    return m_new, l_new, acc_new


def _fox_prompt_kernel(q_ref, k_ref, v_ref, c_ref, ct_ref, o_ref):
    h = pl.program_id(1)
    qi = pl.program_id(2)
    t = ATT_BLOCK
    q = q_ref[...]
    lane = lax.broadcasted_iota(I32, (t, LANES), 1)
    cq = jnp.sum(jnp.where(lane == h, c_ref[...], 0.0), axis=1, keepdims=True)
    sub = lax.broadcasted_iota(I32, (FOX_HEADS, t), 0)

    def scores(j):
        start = pl.multiple_of(j * t, t)
        k = k_ref[pl.ds(start, t), :]
        s = lax.dot_general(q, k, (((1,), (1,)), ((), ())), preferred_element_type=F32)
        ck = jnp.sum(jnp.where(sub == h, ct_ref[0, :, pl.ds(start, t)], 0.0), axis=0, keepdims=True)
        return s + (cq - ck), v_ref[pl.ds(start, t), :]

    def body(j, carry):
        s, v = scores(j)
        return _softmax_update(s, v, *carry)

    init = (jnp.full((t, 1), -jnp.inf, F32), jnp.zeros((t, 1), F32), jnp.zeros((t, FOX_HEAD_DIM), F32))
    carry = lax.fori_loop(0, qi, body, init)
    s, v = scores(qi)
    row = lax.broadcasted_iota(I32, (t, t), 0)
    col = lax.broadcasted_iota(I32, (t, t), 1)
    s = jnp.where(col <= row, s, -jnp.inf)
    _, l, acc = _softmax_update(s, v, *carry)
    o_ref[...] = (acc / l).astype(o_ref.dtype)


def fox_prompt_attention(q, k, v, c, ct, batch, seq):
    nq = seq // ATT_BLOCK
    hd = FOX_HEAD_DIM
    return pl.pallas_call(
        _fox_prompt_kernel,
        grid=(batch, FOX_HEADS, nq),
        in_specs=[
            pl.BlockSpec((ATT_BLOCK, hd), lambda b, h, i: (b * nq + i, h)),
            pl.BlockSpec((seq, hd), lambda b, h, i: (b, h)),
            pl.BlockSpec((seq, hd), lambda b, h, i: (b, h)),
            pl.BlockSpec((ATT_BLOCK, LANES), lambda b, h, i: (b * nq + i, 0)),
            pl.BlockSpec((1, FOX_HEADS, seq), lambda b, h, i: (b, 0, 0)),
        ],
        out_specs=pl.BlockSpec((ATT_BLOCK, hd), lambda b, h, i: (b * nq + i, h)),
        out_shape=jax.ShapeDtypeStruct((batch * seq, FOX_WIDTH), BF16),
        compiler_params=_cparams("arbitrary", "arbitrary", "arbitrary"),
        name="fox_prompt_attention",
    )(q, k, v, c, ct)


FOX_PAGES_PER_STEP = 4
def _expand_heads(x):
    rows = [jnp.broadcast_to(x[h:h + 1, :], (SUBLANES, x.shape[1])) for h in range(FOX_HEADS)]
    return jnp.concatenate(rows, axis=0)


def _fox_sample_kernel(pt_ref, q_ref, *rest, n_steps):
    pp = FOX_PAGES_PER_STEP
    kc_refs = rest[:pp]
    vc_refs = rest[pp:2 * pp]
    lft_refs = rest[2 * pp:3 * pp]
    kn_ref, vn_ref, lfnt_ref, o_ref, qblk, hmask, m_sc, l_sc, acc_sc, carry_sc, cq_sc, cnk_sc = rest[3 * pp:]
    jj = pl.program_id(1)
    rows = FOX_HEADS * SUBLANES
    hd = FOX_HEAD_DIM
    row = lax.broadcasted_iota(I32, (rows, LANES), 0)
    lane = lax.broadcasted_iota(I32, (rows, LANES), 1)
    row_t = row % SUBLANES

    @pl.when(jj == 0)
    def _():
        q = q_ref[...]
        wrow = lax.broadcasted_iota(I32, (rows, FOX_WIDTH), 0)
        wcol = lax.broadcasted_iota(I32, (rows, FOX_WIDTH), 1)
        own = (wrow // SUBLANES) == (wcol // hd)
        qblk[...] = jnp.where(own, jnp.concatenate([q] * FOX_HEADS, axis=0), 0.0).astype(BF16)
        hmask[...] = jnp.where(own, 1.0, 0.0).astype(BF16)
        m_sc[...] = jnp.full(m_sc.shape, -jnp.inf, F32)
        l_sc[...] = jnp.zeros(l_sc.shape, F32)
        acc_sc[...] = jnp.zeros(acc_sc.shape, F32)
        carry_sc[...] = jnp.zeros(carry_sc.shape, F32)
        ui = lax.broadcasted_iota(I32, (LANES, LANES), 0)
        uj = lax.broadcasted_iota(I32, (LANES, LANES), 1)
        cnt = jnp.dot(lfnt_ref[0], (ui <= uj).astype(F32), precision=HIGHEST, preferred_element_type=F32)
        cnk = _expand_heads(cnt)
        cnk_sc[...] = cnk
        cq_sc[...] = jnp.sum(jnp.where(lane == row_t, cnk, 0.0), axis=1, keepdims=True)

    def attend(kmat, vstack, bias, mask):
        s = lax.dot_general(qblk[...], kmat, (((1,), (1,)), ((), ())), preferred_element_type=F32) + bias
        if mask is not None:
            s = jnp.where(mask, s, -jnp.inf)
        m = m_sc[...]
        m_new = jnp.maximum(m, jnp.max(s, axis=-1, keepdims=True))
        alpha = jnp.exp(m - m_new)
        p = jnp.exp(s - m_new)
        l_sc[...] = alpha * l_sc[...] + jnp.sum(p, axis=-1, keepdims=True)
        m_sc[...] = m_new
        pb = p.astype(BF16)
        hm = hmask[...]
        pexp = jnp.concatenate(
            [jnp.concatenate([pb[:, i * PAGE_SIZE:(i + 1) * PAGE_SIZE]] * FOX_HEADS, axis=1) * hm
             for i in range(s.shape[1] // PAGE_SIZE)], axis=1)
        acc_sc[...] = alpha * acc_sc[...] + jnp.dot(pexp, vstack, preferred_element_type=F32)

    ui = lax.broadcasted_iota(I32, (LANES, LANES), 0)
    uj = lax.broadcasted_iota(I32, (LANES, LANES), 1)
    later = (ui > uj).astype(F32)
    head_rows = lambda ref, h: ref[0, pl.ds(h, PAGE_SIZE, stride=FOX_HEADS), :].astype(BF16)
    carry = carry_sc[...]
    biases, kmats, vstacks = [], [], []
    for kc_ref, vc_ref, lft_ref in zip(kc_refs, vc_refs, lft_refs):
        lft = lft_ref[0]
        dt = jnp.dot(lft, later, precision=HIGHEST, preferred_element_type=F32) + carry
        carry = carry + jnp.sum(lft, axis=1, keepdims=True)
        biases.append(_expand_heads(dt))
        kmats.append(jnp.concatenate([head_rows(kc_ref, h) for h in range(FOX_HEADS)], axis=1))
        vstacks.append(jnp.concatenate([head_rows(vc_ref, h) for h in range(FOX_HEADS)], axis=0))
    carry_sc[...] = carry
    attend(jnp.concatenate(kmats, axis=0), jnp.concatenate(vstacks, axis=0),
           jnp.concatenate(biases, axis=1) + cq_sc[...], None)

    @pl.when(jj == n_steps - 1)
    def _():
        zrows = PAGE_SIZE - SUBLANES
        kmat = jnp.concatenate([kn_ref[...], jnp.zeros((zrows, FOX_WIDTH), F32)], axis=0).astype(BF16)
        vn = vn_ref[...]
        zpad = jnp.zeros((zrows, hd), F32)
        vstack = jnp.concatenate(
            [jnp.concatenate([vn[:, h * hd:(h + 1) * hd], zpad], axis=0) for h in range(FOX_HEADS)],
            axis=0).astype(BF16)
        bias = cq_sc[...] - cnk_sc[...]
        mask = (lane < SUBLANES) & (lane <= row_t)
        attend(kmat, vstack, bias, mask)
        out = acc_sc[...] / l_sc[...]
        for h in range(FOX_HEADS):
            o_ref[:, h * hd:(h + 1) * hd] = out[h * SUBLANES:(h + 1) * SUBLANES, :]


def fox_sample_attention(page_table, q, cache_k, cache_v, cache_logf_t, k_new, v_new, logf_new_t):
    bd, n_pages = page_table.shape
    hd = FOX_HEAD_DIM
    rows = FOX_HEADS * SUBLANES

    pp = FOX_PAGES_PER_STEP
    n_steps = n_pages // pp

    def page_map(i, trailing):
        def index(b, j, pt):
            return (pt[b * n_pages + (n_pages - 1 - (j * pp + i))],) + (0,) * trailing
        return index

    row_spec = pl.BlockSpec((SUBLANES, FOX_WIDTH), lambda b, j, pt: (b, 0))
    page_spec = lambda i: pl.BlockSpec((1, PAGE_SIZE * FOX_HEADS, hd), page_map(i, 2))
    pool = cache_k.shape[0]
    cache_k = cache_k.reshape(pool, PAGE_SIZE * FOX_HEADS, hd)
    cache_v = cache_v.reshape(pool, PAGE_SIZE * FOX_HEADS, hd)
    grid_spec = pltpu.PrefetchScalarGridSpec(
        num_scalar_prefetch=1,
        grid=(bd, n_steps),
        in_specs=(
            [row_spec]
            + [page_spec(i) for i in range(pp)]
            + [page_spec(i) for i in range(pp)]
            + [pl.BlockSpec((1, FOX_HEADS, PAGE_SIZE), page_map(i, 2)) for i in range(pp)]
            + [row_spec, row_spec, pl.BlockSpec((1, FOX_HEADS, LANES), lambda b, j, pt: (b, 0, 0))]
        ),
        out_specs=row_spec,
        scratch_shapes=[
            pltpu.VMEM((rows, FOX_WIDTH), BF16),
            pltpu.VMEM((rows, FOX_HEADS * PAGE_SIZE), BF16),
            pltpu.VMEM((rows, 1), F32),
            pltpu.VMEM((rows, 1), F32),
            pltpu.VMEM((rows, hd), F32),
            pltpu.VMEM((FOX_HEADS, 1), F32),
            pltpu.VMEM((rows, 1), F32),
            pltpu.VMEM((rows, LANES), F32),
        ],
    )
    return pl.pallas_call(
        functools.partial(_fox_sample_kernel, n_steps=n_steps),
        grid_spec=grid_spec,
        out_shape=jax.ShapeDtypeStruct((bd * SUBLANES, FOX_WIDTH), F32),
        compiler_params=_cparams("arbitrary", "arbitrary"),
        name="fox_sample_attention",
    )(page_table.reshape(-1), q, *([cache_k] * pp), *([cache_v] * pp), *([cache_logf_t] * pp),
      k_new, v_new, logf_new_t)


def _group_norm_gate(o, gain, gate):
    mu = jnp.mean(o, axis=-1, keepdims=True)
    d = o - mu
    var = jnp.mean(d * d, axis=-1, keepdims=True)
    y = d * lax.rsqrt(var + EPS) * gain
    return y * (gate * jax.nn.sigmoid(gate))


def _ret_prompt_kernel(q_ref, k_ref, v_ref, gr_ref, g_ref, dmat_ref, qdec_ref, kdec_ref, sdec_ref,
                       o_ref, st_ref, state):
    state[...] = jnp.zeros_like(state)
    n_chunks = q_ref.shape[0] // RET_CHUNK
    dmat = dmat_ref[0]
    qdec = qdec_ref[0]
    kdec = kdec_ref[0]
    sdec = sdec_ref[0]
    gain = g_ref[...]

    def body(c, _):
        r = pl.multiple_of(c * RET_CHUNK, RET_CHUNK)
        qc = q_ref[pl.ds(r, RET_CHUNK), :]
        kc = k_ref[pl.ds(r, RET_CHUNK), :]
        vc = v_ref[pl.ds(r, RET_CHUNK), :]
        s = lax.dot_general(qc, kc, (((1,), (1,)), ((), ())), preferred_element_type=F32) * dmat
        st = state[...]
        o = jnp.dot(s.astype(BF16), vc, preferred_element_type=F32)
        o = o + jnp.dot((qc.astype(F32) * qdec).astype(BF16), st.astype(BF16), preferred_element_type=F32)
        kd = (kc.astype(F32) * kdec).astype(BF16)
        state[...] = st * sdec + lax.dot_general(kd, vc, (((0,), (0,)), ((), ())), preferred_element_type=F32)
        o_ref[pl.ds(r, RET_CHUNK), :] = _group_norm_gate(o, gain, gr_ref[pl.ds(r, RET_CHUNK), :]).astype(o_ref.dtype)
        return 0

    lax.fori_loop(0, n_chunks, body, 0)
    st_ref[0, 0] = state[...]


def retention_prompt(qr, kr, vr, gates, ret_norm_g, consts, batch, seq):
    dmat, qdec, kdec, sdec = consts
    blk = lambda b, h: (b, h)
    hconst = lambda b, h: (h, 0, 0)
    return pl.pallas_call(
        _ret_prompt_kernel,
        grid=(batch, RET_HEADS),
        in_specs=[
            pl.BlockSpec((seq, RET_DK), blk),
            pl.BlockSpec((seq, RET_DK), blk),
            pl.BlockSpec((seq, RET_DV), blk),
            pl.BlockSpec((seq, RET_DV), blk),
            pl.BlockSpec((1, RET_DV), lambda b, h: (0, h)),
            pl.BlockSpec((1, RET_CHUNK, RET_CHUNK), hconst),
            pl.BlockSpec((1, RET_CHUNK, RET_DK), hconst),
            pl.BlockSpec((1, RET_CHUNK, RET_DK), hconst),
            pl.BlockSpec((1, 1, RET_DV), hconst),
        ],
        out_specs=[
            pl.BlockSpec((seq, RET_DV), blk),
            pl.BlockSpec((1, 1, RET_DK, RET_DV), lambda b, h: (b, h, 0, 0)),
        ],
        out_shape=[
            jax.ShapeDtypeStruct((batch * seq, RET_WIDTH), BF16),
            jax.ShapeDtypeStruct((batch, RET_HEADS, RET_DK, RET_DV), F32),
        ],
        scratch_shapes=[pltpu.VMEM((RET_DK, RET_DV), F32)],
        compiler_params=_cparams("arbitrary", "arbitrary"),
        name="retention_prompt",
    )(qr, kr, vr, gates, ret_norm_g, dmat, qdec, kdec, sdec)


RET_SAMPLE_Q_ROWS = 16


def _ret_sample_kernel(q_ref, k_ref, v_ref, gr_ref, g_ref, st_in, dmat_ref, qdec_ref, kdec_ref, sdec_ref,
                       o_ref, st_out):
    t = q_ref.shape[0]
    qpad = jnp.zeros((RET_SAMPLE_Q_ROWS - t, RET_DK), F32)
    kpad = jnp.zeros((RET_CHUNK - t, RET_DK), F32)
    for h in range(RET_HEADS):
        cols = slice(h * RET_DK, (h + 1) * RET_DK)
        q = jnp.concatenate([q_ref[:, cols], qpad], axis=0)
        k = jnp.concatenate([k_ref[:, cols], kpad], axis=0)
        v = jnp.concatenate([v_ref[:, cols], kpad], axis=0).astype(BF16)
        st = st_in[0, h]
        s = lax.dot_general(q.astype(BF16), k.astype(BF16), (((1,), (1,)), ((), ())),
                            preferred_element_type=F32) * dmat_ref[h]
        o = jnp.dot(s.astype(BF16), v, preferred_element_type=F32)
        o = o + jnp.dot((q * qdec_ref[h]).astype(BF16), st.astype(BF16), preferred_element_type=F32)
        kd = (k * kdec_ref[h]).astype(BF16)
        st_out[0, h] = st * sdec_ref[h] + lax.dot_general(kd, v, (((0,), (0,)), ((), ())),
                                                         preferred_element_type=F32)
        o_ref[:, cols] = _group_norm_gate(o[:t, :], g_ref[:, cols], gr_ref[:, cols])


def retention_sample(qr, kr, vr, gr, ret_norm_g, state, consts):
    dmat, qdec, kdec, sdec = consts
    bd = state.shape[0]
    t = qr.shape[0] // bd
    row = lambda b: (b, 0)
    const3 = lambda b: (0, 0, 0)
    return pl.pallas_call(
        _ret_sample_kernel,
        grid=(bd,),
        in_specs=[
            pl.BlockSpec((t, RET_WIDTH), row),
            pl.BlockSpec((t, RET_WIDTH), row),
            pl.BlockSpec((t, RET_WIDTH), row),
            pl.BlockSpec((t, RET_WIDTH), row),
            pl.BlockSpec((1, RET_WIDTH), lambda b: (0, 0)),
            pl.BlockSpec((1, RET_HEADS, RET_DK, RET_DV), lambda b: (b, 0, 0, 0)),
            pl.BlockSpec(dmat.shape, const3),
            pl.BlockSpec(qdec.shape, const3),
            pl.BlockSpec(kdec.shape, const3),
            pl.BlockSpec(sdec.shape, const3),
        ],
        out_specs=[
            pl.BlockSpec((t, RET_WIDTH), row),
            pl.BlockSpec((1, RET_HEADS, RET_DK, RET_DV), lambda b: (b, 0, 0, 0)),
        ],
        out_shape=[
            jax.ShapeDtypeStruct((bd * t, RET_WIDTH), F32),
            jax.ShapeDtypeStruct(state.shape, F32),
        ],
        compiler_params=_cparams("arbitrary"),
        name="retention_sample",
    )(qr, kr, vr, gr, ret_norm_g, state, dmat, qdec, kdec, sdec)


def _retention_consts(chunk, q_rows, k_rows):
    lg = jnp.log1p(-jnp.exp2(-5.0 - jnp.arange(RET_HEADS, dtype=F32)))
    n = jnp.arange(chunk, dtype=F32)
    diff = n[:, None] - n[None, :]
    decay = jnp.where(diff[None] >= 0, jnp.exp(jnp.maximum(diff, 0.0)[None] * lg[:, None, None]), 0.0)
    dmat = jnp.zeros((RET_HEADS, q_rows, k_rows), F32).at[:, :chunk, :chunk].set(decay)
    q_dec = jnp.exp((n + 1.0)[None, :] * lg[:, None])
    k_dec = jnp.exp((chunk - 1.0 - n)[None, :] * lg[:, None])
    qdec = jnp.zeros((RET_HEADS, q_rows, RET_DK), F32).at[:, :chunk, :].set(
        jnp.broadcast_to(q_dec[:, :, None], (RET_HEADS, chunk, RET_DK)))
    kdec = jnp.zeros((RET_HEADS, k_rows, RET_DK), F32).at[:, :chunk, :].set(
        jnp.broadcast_to(k_dec[:, :, None], (RET_HEADS, chunk, RET_DK)))
    sdec = jnp.broadcast_to(jnp.exp(chunk * lg)[:, None, None], (RET_HEADS, 1, RET_DV))
    return dmat, qdec, kdec, sdec


ROUTER_LANES = LANES


def _resid_router_kernel(x_ref, mx_ref, g1_ref, sh_ref, sc_ref, g_ref, wr_ref, br_ref,
                         x1_ref, h2_ref, tw_ref, ti_ref):
    x1 = x_ref[...] + g1_ref[...] * mx_ref[...]
    x1_ref[...] = x1
    ms = jnp.mean(x1 * x1, axis=-1, keepdims=True)
    y = x1 * lax.rsqrt(ms + EPS) * g_ref[...]
    h3 = y * (1.0 + sc_ref[...]) + sh_ref[...]
    rows = h3.shape[0] * h3.shape[1]
    h2 = h3.reshape(rows, h3.shape[2])

    half = h2.shape[1] // 2
    bits = pltpu.bitcast(h2.astype(BF16).astype(F32), jnp.uint32)
    h2_ref[...] = (bits[:, :half] >> 16) | (bits[:, half:] & jnp.uint32(0xFFFF0000))

    logits = jnp.dot(h2, wr_ref[...], precision=HIGHEST, preferred_element_type=F32) + br_ref[...]
    lane = lax.broadcasted_iota(I32, logits.shape, 1)
    logits = jnp.where(lane < N_EXPERTS, logits, -jnp.inf)
    vals, idxs = [], []
    for _ in range(TOP_K):
        m = jnp.max(logits, axis=-1, keepdims=True)
        idx = jnp.min(jnp.where(logits == m, lane, ROUTER_LANES), axis=-1, keepdims=True)
        vals.append(m)
        idxs.append(idx)
        logits = jnp.where(lane == idx, -jnp.inf, logits)
    exps = [jnp.exp(v - vals[0]) for v in vals]
    denom = exps[0] + exps[1] + exps[2] + exps[3]
    tw = jnp.zeros(lane.shape, F32)
    ti = jnp.zeros(lane.shape, I32)
    for k in range(TOP_K):
        tw = jnp.where(lane == k, exps[k] / denom, tw)
        ti = jnp.where(lane == k, idxs[k], ti)
    tw_ref[...] = tw
    ti_ref[...] = ti


def resid_norm_router(x3, mixed3, mod4, gate_chunk, shift_chunk, scale_chunk, gain, w_router_pad, b_router_pad,
                      prompt_groups, groups_per_seq):
    g, r, d = x3.shape
    gb = NORM_GROUPS
    rows = gb * r
    n = g * r
    grp = lambda i: (i, 0, 0)
    tok = lambda i: (i, 0)
    mod_spec = functools.partial(_mod_spec, gb=gb, prompt_groups=prompt_groups, groups_per_seq=groups_per_seq, d=d)
    return pl.pallas_call(
        _resid_router_kernel,
        grid=(g // gb,),
        in_specs=[
            pl.BlockSpec((gb, r, d), grp),
            pl.BlockSpec((gb, r, d), grp),
            mod_spec(gate_chunk),
            mod_spec(shift_chunk),
            mod_spec(scale_chunk),
            pl.BlockSpec((1, 1, d), lambda i: (0, 0, 0)),
            pl.BlockSpec((d, ROUTER_LANES), lambda i: (0, 0)),
            pl.BlockSpec((1, ROUTER_LANES), lambda i: (0, 0)),
        ],
        out_specs=[
            pl.BlockSpec((gb, r, d), grp),
            pl.BlockSpec((rows, d // 2), tok),
            pl.BlockSpec((rows, ROUTER_LANES), tok),
            pl.BlockSpec((rows, ROUTER_LANES), tok),
        ],
        out_shape=[
            jax.ShapeDtypeStruct((g, r, d), F32),
            jax.ShapeDtypeStruct((n, d // 2), jnp.uint32),
            jax.ShapeDtypeStruct((n, ROUTER_LANES), F32),
            jax.ShapeDtypeStruct((n, ROUTER_LANES), I32),
        ],
        compiler_params=_cparams("arbitrary"),
        name="resid_norm_router",
    )(x3, mixed3, mod4, mod4, mod4, gain.reshape(1, 1, d), w_router_pad, b_router_pad)


MOE_SUB = 256
MOE_SUBS_PER_TILE = 5
MOE_TILE = MOE_SUB * MOE_SUBS_PER_TILE
FF_TN = 256
DOWN_TN = 512
COMBINE_TOKENS = 64


def _dispatch_kernel(tok_ref, start_ref, nsub_ref, src_hbm, dst_hbm, sem):
    s = pl.program_id(0)
    nsub = nsub_ref[s]
    base = start_ref[s]
    last = tok_ref.shape[0] - 1

    def issue(r, _):
        t = tok_ref[jnp.minimum(base + r, last)]
        pltpu.make_async_copy(src_hbm.at[pl.ds(t, 1), :], dst_hbm.at[pl.ds(s * MOE_TILE + r, 1), :], sem).start()
        return 0

    lax.fori_loop(0, nsub * MOE_SUB, issue, 0)

    def wait(i, _):
        pltpu.make_async_copy(src_hbm.at[pl.ds(0, MOE_SUB), :], dst_hbm.at[pl.ds(0, MOE_SUB), :], sem).wait()
        return 0

    lax.fori_loop(0, nsub, wait, 0)


def moe_dispatch(src_tok, tile_start, st_nsub, h2_packed, m_pad):
    grid_spec = pltpu.PrefetchScalarGridSpec(
        num_scalar_prefetch=3,
        grid=(st_nsub.shape[0],),
        in_specs=[pl.BlockSpec(memory_space=pl.ANY)],
        out_specs=pl.BlockSpec(memory_space=pl.ANY),
        scratch_shapes=[pltpu.SemaphoreType.DMA(())],
    )
    return pl.pallas_call(
        _dispatch_kernel,
        grid_spec=grid_spec,
        out_shape=jax.ShapeDtypeStruct((m_pad, h2_packed.shape[1]), h2_packed.dtype),
        compiler_params=_cparams("arbitrary"),
        name="moe_dispatch",
    )(src_tok, tile_start, st_nsub, h2_packed)


def _unpack_bf16_pairs(words):
    lo = pltpu.bitcast(words << 16, F32).astype(BF16)
    hi = pltpu.bitcast(words & jnp.uint32(0xFFFF0000), F32).astype(BF16)
    return lo, hi


def _moe_up_kernel(e_ref, blk_ref, ns_ref, x_ref, wg_ref, wu_ref, bg_ref, bu_ref, o_ref,
                   xlo, xhi, wg_bf, wu_bf):
    s = pl.program_id(0)
    nsub = ns_ref[s]
    half = xlo.shape[1]

    @pl.when((pl.program_id(1) == 0) & (nsub > 0))
    def _():
        for i in range(MOE_SUBS_PER_TILE):
            @pl.when(i < nsub)
            def _():
                rows = slice(i * MOE_SUB, (i + 1) * MOE_SUB)
                lo, hi = _unpack_bf16_pairs(x_ref[rows, :])
                xlo[rows, :] = lo
                xhi[rows, :] = hi

    @pl.when(nsub > 0)
    def _():
        _cast_rows(wg_ref, wg_bf)
        _cast_rows(wu_ref, wu_bf)
        for i in range(MOE_SUBS_PER_TILE):
            rows = slice(i * MOE_SUB, (i + 1) * MOE_SUB)

            @pl.when(i < nsub)
            def _():
                lo = xlo[rows, :]
                hi = xhi[rows, :]
                g = (jnp.dot(lo, wg_bf[:half, :], preferred_element_type=F32)
                     + jnp.dot(hi, wg_bf[half:, :], preferred_element_type=F32) + bg_ref[...])
                u = (jnp.dot(lo, wu_bf[:half, :], preferred_element_type=F32)
                     + jnp.dot(hi, wu_bf[half:, :], preferred_element_type=F32) + bu_ref[...])
                g = jnp.minimum(g, SWIGLU_LIMIT)
                u = jnp.clip(u, -SWIGLU_LIMIT, SWIGLU_LIMIT)
                o_ref[rows, :] = ((u + 1.0) * g * jax.nn.sigmoid(SWIGLU_ALPHA * g)).astype(o_ref.dtype)

            @pl.when(i >= nsub)
            def _():
                o_ref[rows, :] = jnp.zeros((MOE_SUB, o_ref.shape[1]), o_ref.dtype)


def _frozen_col(n, ns, s, last):
    return jnp.where(ns[s] > 0, n, last)


def moe_up(st_expert, st_block, st_nsub, xs, w_gate_up, b_gate_up):
    m_pad, half = xs.shape
    d = 2 * half
    n_steps = st_expert.shape[0]
    nb = D_FF // FF_TN
    last = nb - 1
    grid_spec = pltpu.PrefetchScalarGridSpec(
        num_scalar_prefetch=3,
        grid=(n_steps, nb),
        in_specs=[
            pl.BlockSpec((MOE_TILE, half), lambda s, n, e, blk, ns: (blk[s], 0)),
            pl.BlockSpec((None, d, FF_TN), lambda s, n, e, blk, ns: (e[s], 0, _frozen_col(n, ns, s, last))),
            pl.BlockSpec((None, d, FF_TN), lambda s, n, e, blk, ns: (e[s], 0, nb + _frozen_col(n, ns, s, last))),
            pl.BlockSpec((None, 1, FF_TN), lambda s, n, e, blk, ns: (e[s], 0, _frozen_col(n, ns, s, last))),
            pl.BlockSpec((None, 1, FF_TN), lambda s, n, e, blk, ns: (e[s], 0, nb + _frozen_col(n, ns, s, last))),
        ],
        out_specs=pl.BlockSpec((MOE_TILE, FF_TN), lambda s, n, e, blk, ns: (blk[s], _frozen_col(n, ns, s, last))),
        scratch_shapes=[
            pltpu.VMEM((MOE_TILE, half), BF16),
            pltpu.VMEM((MOE_TILE, half), BF16),
            pltpu.VMEM((d, FF_TN), BF16),
            pltpu.VMEM((d, FF_TN), BF16),
        ],
    )
    return pl.pallas_call(
        _moe_up_kernel,
        grid_spec=grid_spec,
        out_shape=jax.ShapeDtypeStruct((m_pad, D_FF), BF16),
        compiler_params=_cparams("arbitrary", "arbitrary"),
        name="moe_up",
    )(st_expert, st_block, st_nsub, xs, w_gate_up, w_gate_up, b_gate_up, b_gate_up)


def _moe_down_kernel(e_ref, blk_ref, ns_ref, x_ref, w_ref, b_ref, o_ref, w_bf):
    nsub = ns_ref[pl.program_id(0)]

    @pl.when(nsub > 0)
    def _():
        _cast_rows(w_ref, w_bf)
        for i in range(MOE_SUBS_PER_TILE):
            rows = slice(i * MOE_SUB, (i + 1) * MOE_SUB)

            @pl.when(i < nsub)
            def _():
                o_ref[rows, :] = jnp.dot(x_ref[rows, :], w_bf[...], preferred_element_type=F32) + b_ref[...]

            @pl.when(i >= nsub)
            def _():
                o_ref[rows, :] = jnp.zeros((MOE_SUB, o_ref.shape[1]), o_ref.dtype)


def moe_down(st_expert, st_block, st_nsub, act, w_down, b_down):
    m_pad, dff = act.shape
    d = w_down.shape[2]
    n_steps = st_expert.shape[0]
    nb = d // DOWN_TN
    last = nb - 1
    grid_spec = pltpu.PrefetchScalarGridSpec(
        num_scalar_prefetch=3,
        grid=(n_steps, nb),
        in_specs=[
            pl.BlockSpec((MOE_TILE, dff), lambda s, n, e, blk, ns: (blk[s], 0)),
            pl.BlockSpec((None, dff, DOWN_TN), lambda s, n, e, blk, ns: (e[s], 0, _frozen_col(n, ns, s, last))),
            pl.BlockSpec((None, 1, DOWN_TN), lambda s, n, e, blk, ns: (e[s], 0, _frozen_col(n, ns, s, last))),
        ],
        out_specs=pl.BlockSpec((MOE_TILE, DOWN_TN), lambda s, n, e, blk, ns: (blk[s], _frozen_col(n, ns, s, last))),
        scratch_shapes=[pltpu.VMEM((dff, DOWN_TN), BF16)],
    )
    return pl.pallas_call(
        _moe_down_kernel,
        grid_spec=grid_spec,
        out_shape=jax.ShapeDtypeStruct((m_pad, d), F32),
        compiler_params=_cparams("arbitrary", "arbitrary"),
        name="moe_down",
    )(st_expert, st_block, st_nsub, act, w_down, b_down)


def _combine_kernel(pos_ref, ys_hbm, x1_ref, g2_ref, tw_ref, o_ref, buf, sems):
    i = pl.program_id(0)
    n_steps = pl.num_programs(0)
    t = COMBINE_TOKENS

    def row_copy(slot, k, r, p):
        return pltpu.make_async_copy(ys_hbm.at[pl.ds(p, 1), :], buf.at[slot, k, pl.ds(r, 1), :], sems.at[slot])

    def start(step, slot):
        def issue(r, _):
            for k in range(TOP_K):
                row_copy(slot, k, r, pos_ref[(step * t + r) * TOP_K + k]).start()
            return 0

        lax.fori_loop(0, t, issue, 0)

    @pl.when(i == 0)
    def _():
        start(0, 0)

    @pl.when(i + 1 < n_steps)
    def _():
        start(i + 1, (i + 1) % 2)

    slot = i % 2

    def wait(r, _):
        for k in range(TOP_K):
            row_copy(slot, k, r, 0).wait()
        return 0

    lax.fori_loop(0, t, wait, 0)
    tw = tw_ref[...]
    y = tw[:, 0:1] * buf[slot, 0]
    for k in range(1, TOP_K):
        y = y + tw[:, k:k + 1] * buf[slot, k]
    d = y.shape[1]
    y3 = y.reshape(t // SUBLANES, SUBLANES, d)
    o_ref[...] = x1_ref[...] + g2_ref[...] * y3


def moe_combine(pos, ys, x1_3, mod4, gate_chunk, top_w, prompt_groups, groups_per_seq):
    g, r, d = x1_3.shape
    gb = COMBINE_TOKENS // r
    grid_spec = pltpu.PrefetchScalarGridSpec(
        num_scalar_prefetch=1,
        grid=(g // gb,),
        in_specs=[
            pl.BlockSpec(memory_space=pl.ANY),
            pl.BlockSpec((gb, r, d), lambda i, pos: (i, 0, 0)),
            _mod_spec(gate_chunk, gb, prompt_groups, groups_per_seq, d),
            pl.BlockSpec((COMBINE_TOKENS, ROUTER_LANES), lambda i, pos: (i, 0)),
        ],
        out_specs=pl.BlockSpec((gb, r, d), lambda i, pos: (i, 0, 0)),
        scratch_shapes=[
            pltpu.VMEM((2, TOP_K, COMBINE_TOKENS, d), F32),
            pltpu.SemaphoreType.DMA((2,)),
        ],
    )
    return pl.pallas_call(
        _combine_kernel,
        grid_spec=grid_spec,
        out_shape=jax.ShapeDtypeStruct((g, r, d), F32),
        compiler_params=_cparams("arbitrary"),
        name="moe_combine",
    )(pos, ys, x1_3, mod4, top_w)


def _routing_tables(top_i, n_tiles):
    n = top_i.shape[0]
    m = n * TOP_K
    flat_e = top_i.reshape(m)
    order = jnp.argsort(flat_e, stable=True).astype(I32)
    e_sorted = flat_e[order]
    counts = jnp.sum(jax.nn.one_hot(flat_e, N_EXPERTS, dtype=I32), axis=0)
    starts = jnp.cumsum(counts) - counts
    tiles_per_e = (counts + MOE_TILE - 1) // MOE_TILE
    tile_end = jnp.cumsum(tiles_per_e)
    tile_start = tile_end - tiles_per_e
    total_tiles = tile_end[-1]
    rank = jnp.arange(m, dtype=I32) - starts[e_sorted]
    dest = tile_start[e_sorted] * MOE_TILE + rank
    pos = jnp.zeros((m,), I32).at[order].set(dest)
    src_tok = order // TOP_K
    s = jnp.arange(n_tiles, dtype=I32)
    s_eff = jnp.minimum(s, total_tiles - 1)
    st_expert = jnp.searchsorted(tile_end, s_eff, side="right").astype(I32)
    within = (s_eff - tile_start[st_expert]) * MOE_TILE
    rows_valid = jnp.where(s < total_tiles, jnp.clip(counts[st_expert] - within, 0, MOE_TILE), 0)
    st_nsub = ((rows_valid + MOE_SUB - 1) // MOE_SUB).astype(I32)
    sorted_start = (starts[st_expert] + within).astype(I32)
    return pos, src_tok.astype(I32), sorted_start, st_expert, s_eff.astype(I32), st_nsub


def _rope_tables(pos):
    half = RET_DK // 2
    inv = ROPE_BASE ** (-jnp.arange(half, dtype=F32) / half)
    ang = pos.astype(F32)[:, None] * inv[None, :]
    return jnp.cos(ang), jnp.sin(ang)


def kernel(x_prompt, x_sample, c_prompt, c_sample, cache_k, cache_v, cache_logf, state_ret, page_table, rms1_g, rms2_g, w_ada, b_ada, w_in, b_forget, q_norm_g, k_norm_g, ret_norm_g, w_branch_fox, w_branch_ret, w_out, w_router, b_router, w_gate_up, b_gate_up, w_down, b_down):
    batch, seq, d = x_prompt.shape
    bd, dec_t, _ = x_sample.shape
    depth = w_in.shape[0]
    assert depth == 1 and d == D_MODEL and dec_t == SUBLANES
    n_p = batch * seq
    n_s = bd * dec_t
    n = n_p + n_s
    groups = n // SUBLANES
    past_len = page_table.shape[1] * cache_k.shape[2]

    c_all = jnp.concatenate([jnp.repeat(c_prompt, MOD_REPEAT, axis=0), c_sample], axis=0)
    mod = ada_modulation(c_all, w_ada[0], b_ada[0].reshape(1, -1))
    mod4 = jnp.transpose(mod.reshape(c_all.shape[0], N_MOD, 1, d), (1, 0, 2, 3))
    shift1, scale1, gate1, shift2, scale2, gate2 = range(N_MOD)
    prompt_groups = n_p // SUBLANES
    groups_per_seq = seq // SUBLANES

    x_all = jnp.concatenate([x_prompt.reshape(n_p, d), x_sample.reshape(n_s, d)], axis=0)
    x3 = x_all.reshape(groups, SUBLANES, d)
    h = norm_modulate(x3, mod4, shift1, scale1, rms1_g[0], prompt_groups, groups_per_seq)

    pos = jnp.concatenate([jnp.tile(jnp.arange(seq, dtype=I32), batch),
                           jnp.tile(past_len + jnp.arange(dec_t, dtype=I32), bd)])
    cos, sin = _rope_tables(pos)
    w_in_t = jnp.swapaxes(w_in[0], 0, 1)
    b_forget_pad = jnp.zeros((1, LANES), F32).at[0, :FOX_HEADS].set(b_forget[0])
    q, k32, kbf, v32, vbf, logf, qr, kr, vr, gates = in_projection(
        h, w_in_t, b_forget_pad, q_norm_g[0].reshape(1, -1), k_norm_g[0].reshape(1, -1), cos, sin)

    c_cum, c_cum_t = cumsum_log_forget(logf, batch, seq)
    o_fox_p = fox_prompt_attention(q, kbf, vbf, c_cum, c_cum_t, batch, seq)
    cache_logf_t = jnp.swapaxes(cache_logf[0], 1, 2)
    logf_s = logf[n_p:, :FOX_HEADS].reshape(bd, dec_t, FOX_HEADS)
    logf_new_t = jnp.zeros((bd, FOX_HEADS, LANES), F32).at[:, :, :dec_t].set(jnp.swapaxes(logf_s, 1, 2))
    o_fox_s = fox_sample_attention(page_table, q[n_p:].astype(F32), cache_k[0], cache_v[0], cache_logf_t,
                                   k32[n_p:], v32[n_p:], logf_new_t)
    o_fox = jnp.concatenate([o_fox_p, o_fox_s.astype(BF16)], axis=0)

    g_ret = ret_norm_g[0].reshape(1, -1)
    o_ret_p, state_p = retention_prompt(qr, kr, vr, gates, g_ret,
                                        _retention_consts(RET_CHUNK, RET_CHUNK, RET_CHUNK), batch, seq)
    o_ret_s, state_s = retention_sample(qr[n_p:].astype(F32), kr[n_p:].astype(F32), vr[n_p:].astype(F32),
                                        gates[n_p:, :RET_WIDTH], g_ret, state_ret[0],
                                        _retention_consts(dec_t, RET_SAMPLE_Q_ROWS, RET_CHUNK))
    o_ret = jnp.concatenate([o_ret_p, o_ret_s.astype(BF16)], axis=0)

    gate_blocks = RET_WIDTH // TN
    (mixed_in,) = matmul_ws(
        [o_fox, o_ret], [w_branch_fox[0], w_branch_ret[0]], col_offsets=[0, 0], n_blocks=d // TN,
        epilogue=_ep_branch_merge, extra=(gates, gates),
        extra_specs=(_tile_spec(off=gate_blocks), _tile_spec(off=gate_blocks + d // TN)),
        out_shapes=[jax.ShapeDtypeStruct((n, d), BF16)], out_specs=[_tile_spec()], name="branch_merge")
    (mixed,) = matmul_ws([mixed_in], [w_out[0]], col_offsets=[0], n_blocks=d // TN, epilogue=_ep_plain,
                         out_shapes=[jax.ShapeDtypeStruct((n, d), F32)], out_specs=[_tile_spec()], name="out_proj")

    w_router_pad = jnp.zeros((d, ROUTER_LANES), F32).at[:, :N_EXPERTS].set(w_router[0])
    b_router_pad = jnp.zeros((1, ROUTER_LANES), F32).at[0, :N_EXPERTS].set(b_router[0])
    x1, h2_packed, top_w, top_i = resid_norm_router(
        x3, mixed.reshape(groups, SUBLANES, d), mod4, gate1, shift2, scale2, rms2_g[0], w_router_pad, b_router_pad,
        prompt_groups, groups_per_seq)

    n_tiles = N_EXPERTS + (n * TOP_K) // MOE_TILE
    pos_rows, src_tok, sorted_start, st_expert, st_block, st_nsub = _routing_tables(top_i[:, :TOP_K], n_tiles)
    xs = moe_dispatch(src_tok, sorted_start, st_nsub, h2_packed, n_tiles * MOE_TILE)
    act = moe_up(st_expert, st_block, st_nsub, xs, w_gate_up[0], b_gate_up[0].reshape(N_EXPERTS, 1, -1))
    ys = moe_down(st_expert, st_block, st_nsub, act, w_down[0], b_down[0].reshape(N_EXPERTS, 1, -1))
    y = moe_combine(pos_rows, ys, x1, mod4, gate2, top_w, prompt_groups, groups_per_seq).reshape(n, d)

    y_prompt = y[:n_p].reshape(batch, seq, d)
    y_sample = y[n_p:].reshape(bd, dec_t, d)
    kv_p = (depth, batch, seq, FOX_HEADS, FOX_HEAD_DIM)
    kv_s = (depth, bd, dec_t, FOX_HEADS, FOX_HEAD_DIM)
    logf16 = logf[:, :FOX_HEADS]
    return (y_prompt, y_sample,
            k32[:n_p].reshape(kv_p), v32[:n_p].reshape(kv_p), logf16[:n_p].reshape(depth, batch, seq, FOX_HEADS),
            state_p[None],
            k32[n_p:].reshape(kv_s), v32[n_p:].reshape(kv_s), logf16[n_p:].reshape(depth, bd, dec_t, FOX_HEADS),
            state_s[None])
```

```python
import functools

import jax
import jax.numpy as jnp
import numpy as np
from jax import lax
from jax.experimental import pallas as pl
from jax.experimental.pallas import tpu as pltpu

F32 = jnp.float32
BF16 = jnp.bfloat16
I32 = jnp.int32

D_MODEL = 4096
FOX_HEADS = 16
FOX_HEAD_DIM = 128
FOX_WIDTH = FOX_HEADS * FOX_HEAD_DIM
RET_HEADS = 8
RET_DK = 256
RET_DV = 256
RET_WIDTH = RET_HEADS * RET_DV
RET_CHUNK = 128
ROPE_BASE = 10000.0
N_EXPERTS = 32
TOP_K = 4
D_FF = D_MODEL
SWIGLU_LIMIT = 7.0
SWIGLU_ALPHA = 1.702
EPS = 1e-6
N_MOD = 6
PAGE_SIZE = 128

SUBLANES = 8
LANES = 128
VMEM_LIMIT_BYTES = 56 * 1024 * 1024

TM = 512
TN = 512
CAST_ROWS = 512


def _cparams(*sem):
    return pltpu.CompilerParams(dimension_semantics=sem, vmem_limit_bytes=VMEM_LIMIT_BYTES)


def _cast_rows(src_ref, dst_ref):
    rows = src_ref.shape[0]
    step = min(CAST_ROWS, rows)

    def body(i, _):
        r = pl.multiple_of(i * step, step)
        dst_ref[pl.ds(r, step), :] = src_ref[pl.ds(r, step), :].astype(BF16)
        return 0

    lax.fori_loop(0, rows // step, body, 0)


def _ada_kernel(c_ref, w_ref, b_ref, o_ref, wbf):
    _cast_rows(w_ref, wbf)
    c = c_ref[...]
    a = (c * jax.nn.sigmoid(c)).astype(BF16)
    o_ref[...] = jnp.dot(a, wbf[...], preferred_element_type=F32) + b_ref[...]


def ada_modulation(c_all, w_ada, b_ada):
    rows, d = c_all.shape
    n = w_ada.shape[1]
    return pl.pallas_call(
        _ada_kernel,
        grid=(n // TN,),
        in_specs=[
            pl.BlockSpec((rows, d), lambda j: (0, 0)),
            pl.BlockSpec((d, TN), lambda j: (0, j)),
            pl.BlockSpec((1, TN), lambda j: (0, j)),
        ],
        out_specs=pl.BlockSpec((rows, TN), lambda j: (0, j)),
        out_shape=jax.ShapeDtypeStruct((rows, n), F32),
        scratch_shapes=[pltpu.VMEM((d, TN), BF16)],
        compiler_params=_cparams("arbitrary"),
        name="ada_modulation",
    )(c_all, w_ada, b_ada)


NORM_GROUPS = 16
MOD_REPEAT = 16


def _mod_spec(chunk, gb, prompt_groups, groups_per_seq, d):
    assert MOD_REPEAT % gb == 0 and groups_per_seq % gb == 0 and prompt_groups % gb == 0
    n_prompt_rows = (prompt_groups // groups_per_seq) * MOD_REPEAT

    def index(i, *_):
        g0 = i * gb
        prompt_blk = (g0 // groups_per_seq) * (MOD_REPEAT // gb)
        sample_blk = (n_prompt_rows + g0 - prompt_groups) // gb
        return (chunk, jnp.where(g0 < prompt_groups, prompt_blk, sample_blk), 0, 0)

    return pl.BlockSpec((None, gb, 1, d), index)


def _norm_mod_kernel(x_ref, sh_ref, sc_ref, g_ref, o_ref):
    x = x_ref[...]
    ms = jnp.mean(x * x, axis=-1, keepdims=True)
    y = x * lax.rsqrt(ms + EPS) * g_ref[...]
    h = y * (1.0 + sc_ref[...]) + sh_ref[...]
    o_ref[...] = h.reshape(o_ref.shape).astype(o_ref.dtype)


def norm_modulate(x3, mod4, shift_chunk, scale_chunk, gain, prompt_groups, groups_per_seq):
    g, r, d = x3.shape
    gb = NORM_GROUPS
    mod_spec = functools.partial(_mod_spec, gb=gb, prompt_groups=prompt_groups, groups_per_seq=groups_per_seq, d=d)
    return pl.pallas_call(
        _norm_mod_kernel,
        grid=(g // gb,),
        in_specs=[
            pl.BlockSpec((gb, r, d), lambda i: (i, 0, 0)),
            mod_spec(shift_chunk),
            mod_spec(scale_chunk),
            pl.BlockSpec((1, 1, d), lambda i: (0, 0, 0)),
        ],
        out_specs=pl.BlockSpec((gb * r, d), lambda i: (i, 0)),
        out_shape=jax.ShapeDtypeStruct((g * r, d), BF16),
        compiler_params=_cparams("arbitrary"),
        name="norm_modulate",
    )(x3, mod4, mod4, gain.reshape(1, 1, d))


def _cast_rows_transposed(src_ref, dst_ref):
    k = src_ref.shape[1]
    for c in range(k // CAST_ROWS):
        rows = slice(c * CAST_ROWS, (c + 1) * CAST_ROWS)
        dst_ref[rows, :] = src_ref[:, rows].T.astype(BF16)


def _mm_kernel(*refs, n_a, n_extra, n_out, epilogue, transposed):
    a_refs = refs[:n_a]
    w_refs = refs[n_a:2 * n_a]
    extra = refs[2 * n_a:2 * n_a + n_extra]
    outs = refs[2 * n_a + n_extra:2 * n_a + n_extra + n_out]
    wbfs = refs[2 * n_a + n_extra + n_out:]

    @pl.when(pl.program_id(1) == 0)
    def _():
        for w_ref, wbf in zip(w_refs, wbfs):
            (_cast_rows_transposed if transposed else _cast_rows)(w_ref, wbf)

    accs = [jnp.dot(a[...], wbf[...], preferred_element_type=F32) for a, wbf in zip(a_refs, wbfs)]
    epilogue(accs, extra, outs)


def matmul_ws(a_list, w_list, *, col_offsets, n_blocks, epilogue, out_shapes, out_specs,
              extra=(), extra_specs=(), tn=TN, tm=TM, transposed=False, name):
    m_rows = a_list[0].shape[0]
    in_specs = []
    for a in a_list:
        in_specs.append(pl.BlockSpec((tm, a.shape[1]), lambda n, m: (m, 0)))
    for w, off in zip(w_list, col_offsets):
        if transposed:
            assert off % SUBLANES == 0 and tn % SUBLANES == 0
            in_specs.append(pl.BlockSpec(
                (pl.Element(tn), pl.Element(w.shape[1])),
                functools.partial(lambda n, m, off: (pl.multiple_of(off + n * tn, SUBLANES), 0), off=off)))
        else:
            in_specs.append(pl.BlockSpec((w.shape[0], tn),
                                         functools.partial(lambda n, m, off: (0, n + off), off=off)))
    in_specs += list(extra_specs)
    kernel = functools.partial(_mm_kernel, n_a=len(a_list), n_extra=len(extra), n_out=len(out_shapes),
                               epilogue=epilogue, transposed=transposed)
    k_dims = [w.shape[1] if transposed else w.shape[0] for w in w_list]
    return pl.pallas_call(
        kernel,
        grid=(n_blocks, m_rows // tm),
        in_specs=in_specs,
        out_specs=out_specs,
        out_shape=out_shapes,
        scratch_shapes=[pltpu.VMEM((k, tn), BF16) for k in k_dims],
        compiler_params=_cparams("arbitrary", "arbitrary"),
        name=name,
    )(*a_list, *w_list, *extra)


def _tile_spec(tm=TM, tn=TN, off=0):
    return pl.BlockSpec((tm, tn), functools.partial(lambda n, m, off: (m, n + off), off=off))


def _ep_plain(accs, extra, outs):
    for o in outs:
        o[...] = accs[0].astype(o.dtype)


def _ep_head_norm(accs, extra, outs, *, mult):
    acc = accs[0]
    g = extra[0][...]
    for j in range(acc.shape[1] // FOX_HEAD_DIM):
        z = acc[:, j * FOX_HEAD_DIM:(j + 1) * FOX_HEAD_DIM]
        ms = jnp.mean(z * z, axis=-1, keepdims=True)
        y = z * lax.rsqrt(ms + EPS) * g
        for o, s in zip(outs, mult):
            o[:, j * FOX_HEAD_DIM:(j + 1) * FOX_HEAD_DIM] = (y if s == 1.0 else y * s).astype(o.dtype)


def _ep_log_forget(accs, extra, outs):
    x = accs[0] + extra[0][...]
    lf = jnp.minimum(x, 0.0) - jnp.log1p(jnp.exp(-jnp.abs(x)))
    lane = lax.broadcasted_iota(I32, lf.shape, 1)
    outs[0][...] = jnp.where(lane < FOX_HEADS, lf, 0.0)


def _ep_rope(accs, extra, outs, *, mult):
    acc = accs[0]
    cos = extra[0][...]
    sin = extra[1][...]
    half = RET_DK // 2
    for j in range(acc.shape[1] // RET_DK):
        x1 = acc[:, j * RET_DK:j * RET_DK + half]
        x2 = acc[:, j * RET_DK + half:(j + 1) * RET_DK]
        y1 = x1 * cos - x2 * sin
        y2 = x2 * cos + x1 * sin
        if mult != 1.0:
            y1 = y1 * mult
            y2 = y2 * mult
        outs[0][:, j * RET_DK:j * RET_DK + half] = y1.astype(outs[0].dtype)
        outs[0][:, j * RET_DK + half:(j + 1) * RET_DK] = y2.astype(outs[0].dtype)


def _ep_branch_merge(accs, extra, outs):
    gf = jax.nn.sigmoid(extra[0][...])
    gr = jax.nn.sigmoid(extra[1][...])
    outs[0][...] = (gf * accs[0] + gr * accs[1]).astype(outs[0].dtype)


IN_FL_START = 3 * FOX_WIDTH
IN_TAIL_START = IN_FL_START + FOX_HEADS
GATE_COLS = RET_WIDTH + 2 * D_MODEL


def in_projection(h, w_in_t, b_forget_pad, q_norm_g, k_norm_g, cos, sin):
    n = h.shape[0]
    sds = jax.ShapeDtypeStruct
    wide = FOX_WIDTH // TN
    g_spec = pl.BlockSpec((1, FOX_HEAD_DIM), lambda j, m: (0, 0))
    rope_specs = [pl.BlockSpec((TM, RET_DK // 2), lambda j, m: (m, 0))] * 2
    mm = functools.partial(matmul_ws, [h], [w_in_t], transposed=True)

    (q,) = mm(col_offsets=[0], n_blocks=wide,
              epilogue=functools.partial(_ep_head_norm, mult=(FOX_HEAD_DIM ** -0.5,)),
              extra=(q_norm_g,), extra_specs=(g_spec,),
              out_shapes=[sds((n, FOX_WIDTH), BF16)], out_specs=[_tile_spec()], name="in_proj_fox_q")
    k32, kbf = mm(col_offsets=[FOX_WIDTH], n_blocks=wide,
                  epilogue=functools.partial(_ep_head_norm, mult=(1.0, 1.0)),
                  extra=(k_norm_g,), extra_specs=(g_spec,),
                  out_shapes=[sds((n, FOX_WIDTH), F32), sds((n, FOX_WIDTH), BF16)],
                  out_specs=[_tile_spec(), _tile_spec()], name="in_proj_fox_k")
    v32, vbf = mm(col_offsets=[2 * FOX_WIDTH], n_blocks=wide, epilogue=_ep_plain,
                  out_shapes=[sds((n, FOX_WIDTH), F32), sds((n, FOX_WIDTH), BF16)],
                  out_specs=[_tile_spec(), _tile_spec()], name="in_proj_fox_v")
    (logf,) = mm(col_offsets=[IN_FL_START], n_blocks=1, tn=LANES,
                 epilogue=_ep_log_forget, extra=(b_forget_pad,),
                 extra_specs=(pl.BlockSpec((1, LANES), lambda j, m: (0, 0)),),
                 out_shapes=[sds((n, LANES), F32)], out_specs=[_tile_spec(tn=LANES)],
                 name="in_proj_log_forget")
    (qr,) = mm(col_offsets=[IN_TAIL_START], n_blocks=wide,
               epilogue=functools.partial(_ep_rope, mult=1.0), extra=(cos, sin), extra_specs=rope_specs,
               out_shapes=[sds((n, RET_WIDTH), BF16)], out_specs=[_tile_spec()], name="in_proj_ret_q")
    (kr,) = mm(col_offsets=[IN_TAIL_START + RET_WIDTH], n_blocks=wide,
               epilogue=functools.partial(_ep_rope, mult=RET_DK ** -0.5), extra=(cos, sin),
               extra_specs=rope_specs,
               out_shapes=[sds((n, RET_WIDTH), BF16)], out_specs=[_tile_spec()], name="in_proj_ret_k")
    (vr,) = mm(col_offsets=[IN_TAIL_START + 2 * RET_WIDTH], n_blocks=wide, epilogue=_ep_plain,
               out_shapes=[sds((n, RET_WIDTH), BF16)], out_specs=[_tile_spec()], name="in_proj_ret_v")
    (gates,) = mm(col_offsets=[IN_TAIL_START + 3 * RET_WIDTH], n_blocks=GATE_COLS // TN, epilogue=_ep_plain,
                  out_shapes=[sds((n, GATE_COLS), F32)], out_specs=[_tile_spec()], name="in_proj_gates")
    return q, k32, kbf, v32, vbf, logf, qr, kr, vr, gates


CUMSUM_BLOCK = 256
HIGHEST = lax.Precision.HIGHEST


def _cumsum_kernel(lf_ref, c_ref, ct_ref, carry):
    @pl.when(pl.program_id(1) == 0)
    def _():
        carry[...] = jnp.zeros_like(carry)

    blk = lf_ref.shape[0]
    r = lax.broadcasted_iota(I32, (blk, blk), 0)
    c = lax.broadcasted_iota(I32, (blk, blk), 1)
    lower = (c <= r).astype(F32)
    cs = jnp.dot(lower, lf_ref[...], precision=HIGHEST, preferred_element_type=F32) + carry[0:1, :]
    c_ref[...] = cs
    er = lax.broadcasted_iota(I32, (FOX_HEADS, LANES), 0)
    ec = lax.broadcasted_iota(I32, (FOX_HEADS, LANES), 1)
    eye = (er == ec).astype(F32)
    ct_ref[0] = lax.dot_general(eye, cs, (((1,), (1,)), ((), ())), precision=HIGHEST,
                                preferred_element_type=F32)
    carry[...] = jnp.broadcast_to(cs[blk - 1:blk, :], carry.shape)


def cumsum_log_forget(logf_pad, batch, seq):
    nb = seq // CUMSUM_BLOCK
    return pl.pallas_call(
        _cumsum_kernel,
        grid=(batch, nb),
        in_specs=[pl.BlockSpec((CUMSUM_BLOCK, LANES), lambda b, s: (b * nb + s, 0))],
        out_specs=[
            pl.BlockSpec((CUMSUM_BLOCK, LANES), lambda b, s: (b * nb + s, 0)),
            pl.BlockSpec((1, FOX_HEADS, CUMSUM_BLOCK), lambda b, s: (b, 0, s)),
        ],
        out_shape=[
            jax.ShapeDtypeStruct((batch * seq, LANES), F32),
            jax.ShapeDtypeStruct((batch, FOX_HEADS, seq), F32),
        ],
        scratch_shapes=[pltpu.VMEM((SUBLANES, LANES), F32)],
        compiler_params=_cparams("arbitrary", "arbitrary"),
        name="cumsum_log_forget",
    )(logf_pad)


ATT_BLOCK = 512


def _softmax_update(s, v, m, l, acc):
    m_new = jnp.maximum(m, jnp.max(s, axis=-1, keepdims=True))
    alpha = jnp.exp(m - m_new)
    p = jnp.exp(s - m_new)
    l_new = alpha * l + jnp.sum(p, axis=-1, keepdims=True)
    acc_new = alpha * acc + jnp.dot(p.astype(BF16), v, preferred_element_type=F32)
    return m_new, l_new, acc_new


def _fox_prompt_kernel(q_ref, k_ref, v_ref, c_ref, ct_ref, o_ref):
    h = pl.program_id(1)
    qi = pl.program_id(2)
    t = ATT_BLOCK
    q = q_ref[...]
    lane = lax.broadcasted_iota(I32, (t, LANES), 1)
    cq = jnp.sum(jnp.where(lane == h, c_ref[...], 0.0), axis=1, keepdims=True)
    sub = lax.broadcasted_iota(I32, (FOX_HEADS, t), 0)

    def scores(j):
        start = pl.multiple_of(j * t, t)
        k = k_ref[pl.ds(start, t), :]
        s = lax.dot_general(q, k, (((1,), (1,)), ((), ())), preferred_element_type=F32)
        ck = jnp.sum(jnp.where(sub == h, ct_ref[0, :, pl.ds(start, t)], 0.0), axis=0, keepdims=True)
        return s + (cq - ck), v_ref[pl.ds(start, t), :]

    def body(j, carry):
        s, v = scores(j)
        return _softmax_update(s, v, *carry)

    init = (jnp.full((t, 1), -jnp.inf, F32), jnp.zeros((t, 1), F32), jnp.zeros((t, FOX_HEAD_DIM), F32))
    carry = lax.fori_loop(0, qi, body, init)
    s, v = scores(qi)
    row = lax.broadcasted_iota(I32, (t, t), 0)
    col = lax.broadcasted_iota(I32, (t, t), 1)
    s = jnp.where(col <= row, s, -jnp.inf)
    _, l, acc = _softmax_update(s, v, *carry)
    o_ref[...] = (acc / l).astype(o_ref.dtype)


def fox_prompt_attention(q, k, v, c, ct, batch, seq):
    nq = seq // ATT_BLOCK
    hd = FOX_HEAD_DIM
    return pl.pallas_call(
        _fox_prompt_kernel,
        grid=(batch, FOX_HEADS, nq),
        in_specs=[
            pl.BlockSpec((ATT_BLOCK, hd), lambda b, h, i: (b * nq + i, h)),
            pl.BlockSpec((seq, hd), lambda b, h, i: (b, h)),
            pl.BlockSpec((seq, hd), lambda b, h, i: (b, h)),
            pl.BlockSpec((ATT_BLOCK, LANES), lambda b, h, i: (b * nq + i, 0)),
            pl.BlockSpec((1, FOX_HEADS, seq), lambda b, h, i: (b, 0, 0)),
        ],
        out_specs=pl.BlockSpec((ATT_BLOCK, hd), lambda b, h, i: (b * nq + i, h)),
        out_shape=jax.ShapeDtypeStruct((batch * seq, FOX_WIDTH), BF16),
        compiler_params=_cparams("arbitrary", "arbitrary", "arbitrary"),
        name="fox_prompt_attention",
    )(q, k, v, c, ct)


FOX_PAGES_PER_STEP = 4
def _expand_heads(x):
    rows = [jnp.broadcast_to(x[h:h + 1, :], (SUBLANES, x.shape[1])) for h in range(FOX_HEADS)]
    return jnp.concatenate(rows, axis=0)


def _fox_sample_kernel(pt_ref, q_ref, *rest, n_steps):
    pp = FOX_PAGES_PER_STEP
    kc_refs = rest[:pp]
    vc_refs = rest[pp:2 * pp]
    lft_refs = rest[2 * pp:3 * pp]
    kn_ref, vn_ref, lfnt_ref, o_ref, qblk, hmask, m_sc, l_sc, acc_sc, carry_sc, cq_sc, cnk_sc = rest[3 * pp:]
    jj = pl.program_id(1)
    rows = FOX_HEADS * SUBLANES
    hd = FOX_HEAD_DIM
    row = lax.broadcasted_iota(I32, (rows, LANES), 0)
    lane = lax.broadcasted_iota(I32, (rows, LANES), 1)
    row_t = row % SUBLANES

    @pl.when(jj == 0)
    def _():
        q = q_ref[...]
        wrow = lax.broadcasted_iota(I32, (rows, FOX_WIDTH), 0)
        wcol = lax.broadcasted_iota(I32, (rows, FOX_WIDTH), 1)
        own = (wrow // SUBLANES) == (wcol // hd)
        qblk[...] = jnp.where(own, jnp.concatenate([q] * FOX_HEADS, axis=0), 0.0).astype(BF16)
        hmask[...] = jnp.where(own, 1.0, 0.0).astype(BF16)
        m_sc[...] = jnp.full(m_sc.shape, -jnp.inf, F32)
        l_sc[...] = jnp.zeros(l_sc.shape, F32)
        acc_sc[...] = jnp.zeros(acc_sc.shape, F32)
        carry_sc[...] = jnp.zeros(carry_sc.shape, F32)
        ui = lax.broadcasted_iota(I32, (LANES, LANES), 0)
        uj = lax.broadcasted_iota(I32, (LANES, LANES), 1)
        cnt = jnp.dot(lfnt_ref[0], (ui <= uj).astype(F32), precision=HIGHEST, preferred_element_type=F32)
        cnk = _expand_heads(cnt)
        cnk_sc[...] = cnk
        cq_sc[...] = jnp.sum(jnp.where(lane == row_t, cnk, 0.0), axis=1, keepdims=True)

    def attend(kmat, vstack, bias, mask):
        s = lax.dot_general(qblk[...], kmat, (((1,), (1,)), ((), ())), preferred_element_type=F32) + bias
        if mask is not None:
            s = jnp.where(mask, s, -jnp.inf)
        m = m_sc[...]
        m_new = jnp.maximum(m, jnp.max(s, axis=-1, keepdims=True))
        alpha = jnp.exp(m - m_new)
        p = jnp.exp(s - m_new)
        l_sc[...] = alpha * l_sc[...] + jnp.sum(p, axis=-1, keepdims=True)
        m_sc[...] = m_new
        pb = p.astype(BF16)
        hm = hmask[...]
        pexp = jnp.concatenate(
            [jnp.concatenate([pb[:, i * PAGE_SIZE:(i + 1) * PAGE_SIZE]] * FOX_HEADS, axis=1) * hm
             for i in range(s.shape[1] // PAGE_SIZE)], axis=1)
        acc_sc[...] = alpha * acc_sc[...] + jnp.dot(pexp, vstack, preferred_element_type=F32)

    ui = lax.broadcasted_iota(I32, (LANES, LANES), 0)
    uj = lax.broadcasted_iota(I32, (LANES, LANES), 1)
    later = (ui > uj).astype(F32)
    head_rows = lambda ref, h: ref[0, pl.ds(h, PAGE_SIZE, stride=FOX_HEADS), :].astype(BF16)
    carry = carry_sc[...]
    biases, kmats, vstacks = [], [], []
    for kc_ref, vc_ref, lft_ref in zip(kc_refs, vc_refs, lft_refs):
        lft = lft_ref[0]
        dt = jnp.dot(lft, later, precision=HIGHEST, preferred_element_type=F32) + carry
        carry = carry + jnp.sum(lft, axis=1, keepdims=True)
        biases.append(_expand_heads(dt))
        kmats.append(jnp.concatenate([head_rows(kc_ref, h) for h in range(FOX_HEADS)], axis=1))
        vstacks.append(jnp.concatenate([head_rows(vc_ref, h) for h in range(FOX_HEADS)], axis=0))
    carry_sc[...] = carry
    attend(jnp.concatenate(kmats, axis=0), jnp.concatenate(vstacks, axis=0),
           jnp.concatenate(biases, axis=1) + cq_sc[...], None)

    @pl.when(jj == n_steps - 1)
    def _():
        zrows = PAGE_SIZE - SUBLANES
        kmat = jnp.concatenate([kn_ref[...], jnp.zeros((zrows, FOX_WIDTH), F32)], axis=0).astype(BF16)
        vn = vn_ref[...]
        zpad = jnp.zeros((zrows, hd), F32)
        vstack = jnp.concatenate(
            [jnp.concatenate([vn[:, h * hd:(h + 1) * hd], zpad], axis=0) for h in range(FOX_HEADS)],
            axis=0).astype(BF16)
        bias = cq_sc[...] - cnk_sc[...]
        mask = (lane < SUBLANES) & (lane <= row_t)
        attend(kmat, vstack, bias, mask)
        out = acc_sc[...] / l_sc[...]
        for h in range(FOX_HEADS):
            o_ref[:, h * hd:(h + 1) * hd] = out[h * SUBLANES:(h + 1) * SUBLANES, :]


def fox_sample_attention(page_table, q, cache_k, cache_v, cache_logf_t, k_new, v_new, logf_new_t):
    bd, n_pages = page_table.shape
    hd = FOX_HEAD_DIM
    rows = FOX_HEADS * SUBLANES

    pp = FOX_PAGES_PER_STEP
    n_steps = n_pages // pp

    def page_map(i, trailing):
        def index(b, j, pt):
            return (pt[b * n_pages + (n_pages - 1 - (j * pp + i))],) + (0,) * trailing
        return index

    row_spec = pl.BlockSpec((SUBLANES, FOX_WIDTH), lambda b, j, pt: (b, 0))
    page_spec = lambda i: pl.BlockSpec((1, PAGE_SIZE * FOX_HEADS, hd), page_map(i, 2))
    pool = cache_k.shape[0]
    cache_k = cache_k.reshape(pool, PAGE_SIZE * FOX_HEADS, hd)
    cache_v = cache_v.reshape(pool, PAGE_SIZE * FOX_HEADS, hd)
    grid_spec = pltpu.PrefetchScalarGridSpec(
        num_scalar_prefetch=1,
        grid=(bd, n_steps),
        in_specs=(
            [row_spec]
            + [page_spec(i) for i in range(pp)]
            + [page_spec(i) for i in range(pp)]
            + [pl.BlockSpec((1, FOX_HEADS, PAGE_SIZE), page_map(i, 2)) for i in range(pp)]
            + [row_spec, row_spec, pl.BlockSpec((1, FOX_HEADS, LANES), lambda b, j, pt: (b, 0, 0))]
        ),
        out_specs=row_spec,
        scratch_shapes=[
            pltpu.VMEM((rows, FOX_WIDTH), BF16),
            pltpu.VMEM((rows, FOX_HEADS * PAGE_SIZE), BF16),
            pltpu.VMEM((rows, 1), F32),
            pltpu.VMEM((rows, 1), F32),
            pltpu.VMEM((rows, hd), F32),
            pltpu.VMEM((FOX_HEADS, 1), F32),
            pltpu.VMEM((rows, 1), F32),
            pltpu.VMEM((rows, LANES), F32),
        ],
    )
    return pl.pallas_call(
        functools.partial(_fox_sample_kernel, n_steps=n_steps),
        grid_spec=grid_spec,
        out_shape=jax.ShapeDtypeStruct((bd * SUBLANES, FOX_WIDTH), F32),
        compiler_params=_cparams("arbitrary", "arbitrary"),
        name="fox_sample_attention",
    )(page_table.reshape(-1), q, *([cache_k] * pp), *([cache_v] * pp), *([cache_logf_t] * pp),
      k_new, v_new, logf_new_t)


def _group_norm_gate(o, gain, gate):
    mu = jnp.mean(o, axis=-1, keepdims=True)
    d = o - mu
    var = jnp.mean(d * d, axis=-1, keepdims=True)
    y = d * lax.rsqrt(var + EPS) * gain
    return y * (gate * jax.nn.sigmoid(gate))


def _ret_prompt_kernel(q_ref, k_ref, v_ref, gr_ref, g_ref, dmat_ref, qdec_ref, kdec_ref, sdec_ref,
                       o_ref, st_ref, state):
    state[...] = jnp.zeros_like(state)
    n_chunks = q_ref.shape[0] // RET_CHUNK
    dmat = dmat_ref[0]
    qdec = qdec_ref[0]
    kdec = kdec_ref[0]
    sdec = sdec_ref[0]
    gain = g_ref[...]

    def body(c, _):
        r = pl.multiple_of(c * RET_CHUNK, RET_CHUNK)
        qc = q_ref[pl.ds(r, RET_CHUNK), :]
        kc = k_ref[pl.ds(r, RET_CHUNK), :]
        vc = v_ref[pl.ds(r, RET_CHUNK), :]
        s = lax.dot_general(qc, kc, (((1,), (1,)), ((), ())), preferred_element_type=F32) * dmat
        st = state[...]
        o = jnp.dot(s.astype(BF16), vc, preferred_element_type=F32)
        o = o + jnp.dot((qc.astype(F32) * qdec).astype(BF16), st.astype(BF16), preferred_element_type=F32)
        kd = (kc.astype(F32) * kdec).astype(BF16)
        state[...] = st * sdec + lax.dot_general(kd, vc, (((0,), (0,)), ((), ())), preferred_element_type=F32)
        o_ref[pl.ds(r, RET_CHUNK), :] = _group_norm_gate(o, gain, gr_ref[pl.ds(r, RET_CHUNK), :]).astype(o_ref.dtype)
        return 0

    lax.fori_loop(0, n_chunks, body, 0)
    st_ref[0, 0] = state[...]


def retention_prompt(qr, kr, vr, gates, ret_norm_g, consts, batch, seq):
    dmat, qdec, kdec, sdec = consts
    blk = lambda b, h: (b, h)
    hconst = lambda b, h: (h, 0, 0)
    return pl.pallas_call(
        _ret_prompt_kernel,
        grid=(batch, RET_HEADS),
        in_specs=[
            pl.BlockSpec((seq, RET_DK), blk),
            pl.BlockSpec((seq, RET_DK), blk),
            pl.BlockSpec((seq, RET_DV), blk),
            pl.BlockSpec((seq, RET_DV), blk),
            pl.BlockSpec((1, RET_DV), lambda b, h: (0, h)),
            pl.BlockSpec((1, RET_CHUNK, RET_CHUNK), hconst),
            pl.BlockSpec((1, RET_CHUNK, RET_DK), hconst),
            pl.BlockSpec((1, RET_CHUNK, RET_DK), hconst),
            pl.BlockSpec((1, 1, RET_DV), hconst),
        ],
        out_specs=[
            pl.BlockSpec((seq, RET_DV), blk),
            pl.BlockSpec((1, 1, RET_DK, RET_DV), lambda b, h: (b, h, 0, 0)),
        ],
        out_shape=[
            jax.ShapeDtypeStruct((batch * seq, RET_WIDTH), BF16),
            jax.ShapeDtypeStruct((batch, RET_HEADS, RET_DK, RET_DV), F32),
        ],
        scratch_shapes=[pltpu.VMEM((RET_DK, RET_DV), F32)],
        compiler_params=_cparams("arbitrary", "arbitrary"),
        name="retention_prompt",
    )(qr, kr, vr, gates, ret_norm_g, dmat, qdec, kdec, sdec)


RET_SAMPLE_Q_ROWS = 16


def _ret_sample_kernel(q_ref, k_ref, v_ref, gr_ref, g_ref, st_in, dmat_ref, qdec_ref, kdec_ref, sdec_ref,
                       o_ref, st_out):
    t = q_ref.shape[0]
    qpad = jnp.zeros((RET_SAMPLE_Q_ROWS - t, RET_DK), F32)
    kpad = jnp.zeros((RET_CHUNK - t, RET_DK), F32)
    for h in range(RET_HEADS):
        cols = slice(h * RET_DK, (h + 1) * RET_DK)
        q = jnp.concatenate([q_ref[:, cols], qpad], axis=0)
        k = jnp.concatenate([k_ref[:, cols], kpad], axis=0)
        v = jnp.concatenate([v_ref[:, cols], kpad], axis=0).astype(BF16)
        st = st_in[0, h]
        s = lax.dot_general(q.astype(BF16), k.astype(BF16), (((1,), (1,)), ((), ())),
                            preferred_element_type=F32) * dmat_ref[h]
        o = jnp.dot(s.astype(BF16), v, preferred_element_type=F32)
        o = o + jnp.dot((q * qdec_ref[h]).astype(BF16), st.astype(BF16), preferred_element_type=F32)
        kd = (k * kdec_ref[h]).astype(BF16)
        st_out[0, h] = st * sdec_ref[h] + lax.dot_general(kd, v, (((0,), (0,)), ((), ())),
                                                         preferred_element_type=F32)
        o_ref[:, cols] = _group_norm_gate(o[:t, :], g_ref[:, cols], gr_ref[:, cols])


def retention_sample(qr, kr, vr, gr, ret_norm_g, state, consts):
    dmat, qdec, kdec, sdec = consts
    bd = state.shape[0]
    t = qr.shape[0] // bd
    row = lambda b: (b, 0)
    const3 = lambda b: (0, 0, 0)
    return pl.pallas_call(
        _ret_sample_kernel,
        grid=(bd,),
        in_specs=[
            pl.BlockSpec((t, RET_WIDTH), row),
            pl.BlockSpec((t, RET_WIDTH), row),
            pl.BlockSpec((t, RET_WIDTH), row),
            pl.BlockSpec((t, RET_WIDTH), row),
            pl.BlockSpec((1, RET_WIDTH), lambda b: (0, 0)),
            pl.BlockSpec((1, RET_HEADS, RET_DK, RET_DV), lambda b: (b, 0, 0, 0)),
            pl.BlockSpec(dmat.shape, const3),
            pl.BlockSpec(qdec.shape, const3),
            pl.BlockSpec(kdec.shape, const3),
            pl.BlockSpec(sdec.shape, const3),
        ],
        out_specs=[
            pl.BlockSpec((t, RET_WIDTH), row),
            pl.BlockSpec((1, RET_HEADS, RET_DK, RET_DV), lambda b: (b, 0, 0, 0)),
        ],
        out_shape=[
            jax.ShapeDtypeStruct((bd * t, RET_WIDTH), F32),
            jax.ShapeDtypeStruct(state.shape, F32),
        ],
        compiler_params=_cparams("arbitrary"),
        name="retention_sample",
    )(qr, kr, vr, gr, ret_norm_g, state, dmat, qdec, kdec, sdec)


def _retention_consts(chunk, q_rows, k_rows):
    lg = jnp.log1p(-jnp.exp2(-5.0 - jnp.arange(RET_HEADS, dtype=F32)))
    n = jnp.arange(chunk, dtype=F32)
    diff = n[:, None] - n[None, :]
    decay = jnp.where(diff[None] >= 0, jnp.exp(jnp.maximum(diff, 0.0)[None] * lg[:, None, None]), 0.0)
    dmat = jnp.zeros((RET_HEADS, q_rows, k_rows), F32).at[:, :chunk, :chunk].set(decay)
    q_dec = jnp.exp((n + 1.0)[None, :] * lg[:, None])
    k_dec = jnp.exp((chunk - 1.0 - n)[None, :] * lg[:, None])
    qdec = jnp.zeros((RET_HEADS, q_rows, RET_DK), F32).at[:, :chunk, :].set(
        jnp.broadcast_to(q_dec[:, :, None], (RET_HEADS, chunk, RET_DK)))
    kdec = jnp.zeros((RET_HEADS, k_rows, RET_DK), F32).at[:, :chunk, :].set(
        jnp.broadcast_to(k_dec[:, :, None], (RET_HEADS, chunk, RET_DK)))
    sdec = jnp.broadcast_to(jnp.exp(chunk * lg)[:, None, None], (RET_HEADS, 1, RET_DV))
    return dmat, qdec, kdec, sdec


ROUTER_LANES = LANES


def _resid_router_kernel(x_ref, mx_ref, g1_ref, sh_ref, sc_ref, g_ref, wr_ref, br_ref,
                         x1_ref, h2_ref, tw_ref, ti_ref):
    x1 = x_ref[...] + g1_ref[...] * mx_ref[...]
    x1_ref[...] = x1
    ms = jnp.mean(x1 * x1, axis=-1, keepdims=True)
    y = x1 * lax.rsqrt(ms + EPS) * g_ref[...]
    h3 = y * (1.0 + sc_ref[...]) + sh_ref[...]
    rows = h3.shape[0] * h3.shape[1]
    h2 = h3.reshape(rows, h3.shape[2])

    half = h2.shape[1] // 2
    bits = pltpu.bitcast(h2.astype(BF16).astype(F32), jnp.uint32)
    h2_ref[...] = (bits[:, :half] >> 16) | (bits[:, half:] & jnp.uint32(0xFFFF0000))

    logits = jnp.dot(h2, wr_ref[...], precision=HIGHEST, preferred_element_type=F32) + br_ref[...]
    lane = lax.broadcasted_iota(I32, logits.shape, 1)
    logits = jnp.where(lane < N_EXPERTS, logits, -jnp.inf)
    vals, idxs = [], []
    for _ in range(TOP_K):
        m = jnp.max(logits, axis=-1, keepdims=True)
        idx = jnp.min(jnp.where(logits == m, lane, ROUTER_LANES), axis=-1, keepdims=True)
        vals.append(m)
        idxs.append(idx)
        logits = jnp.where(lane == idx, -jnp.inf, logits)
    exps = [jnp.exp(v - vals[0]) for v in vals]
    denom = exps[0] + exps[1] + exps[2] + exps[3]
    tw = jnp.zeros(lane.shape, F32)
    ti = jnp.zeros(lane.shape, I32)
    for k in range(TOP_K):
        tw = jnp.where(lane == k, exps[k] / denom, tw)
        ti = jnp.where(lane == k, idxs[k], ti)
    tw_ref[...] = tw
    ti_ref[...] = ti


def resid_norm_router(x3, mixed3, mod4, gate_chunk, shift_chunk, scale_chunk, gain, w_router_pad, b_router_pad,
                      prompt_groups, groups_per_seq):
    g, r, d = x3.shape
    gb = NORM_GROUPS
    rows = gb * r
    n = g * r
    grp = lambda i: (i, 0, 0)
    tok = lambda i: (i, 0)
    mod_spec = functools.partial(_mod_spec, gb=gb, prompt_groups=prompt_groups, groups_per_seq=groups_per_seq, d=d)
    return pl.pallas_call(
        _resid_router_kernel,
        grid=(g // gb,),
        in_specs=[
            pl.BlockSpec((gb, r, d), grp),
            pl.BlockSpec((gb, r, d), grp),
            mod_spec(gate_chunk),
            mod_spec(shift_chunk),
            mod_spec(scale_chunk),
            pl.BlockSpec((1, 1, d), lambda i: (0, 0, 0)),
            pl.BlockSpec((d, ROUTER_LANES), lambda i: (0, 0)),
            pl.BlockSpec((1, ROUTER_LANES), lambda i: (0, 0)),
        ],
        out_specs=[
            pl.BlockSpec((gb, r, d), grp),
            pl.BlockSpec((rows, d // 2), tok),
            pl.BlockSpec((rows, ROUTER_LANES), tok),
            pl.BlockSpec((rows, ROUTER_LANES), tok),
        ],
        out_shape=[
            jax.ShapeDtypeStruct((g, r, d), F32),
            jax.ShapeDtypeStruct((n, d // 2), jnp.uint32),
            jax.ShapeDtypeStruct((n, ROUTER_LANES), F32),
            jax.ShapeDtypeStruct((n, ROUTER_LANES), I32),
        ],
        compiler_params=_cparams("arbitrary"),
        name="resid_norm_router",
    )(x3, mixed3, mod4, mod4, mod4, gain.reshape(1, 1, d), w_router_pad, b_router_pad)


MOE_SUB = 256
MOE_SUBS_PER_TILE = 5
MOE_TILE = MOE_SUB * MOE_SUBS_PER_TILE
FF_TN = 256
DOWN_TN = 512
COMBINE_TOKENS = 64


def _dispatch_kernel(tok_ref, start_ref, nsub_ref, ob_ref, src_hbm, o_ref, sem):
    s = pl.program_id(0)
    i = pl.program_id(1)
    base = start_ref[s] + i * MOE_SUB
    last = tok_ref.shape[0] - 1

    @pl.when(i < nsub_ref[s])
    def _():
        def issue(r, _):
            t = tok_ref[jnp.minimum(base + r, last)]
            pltpu.make_async_copy(src_hbm.at[pl.ds(t, 1), :], o_ref.at[pl.ds(r, 1), :], sem).start()
            return 0

        lax.fori_loop(0, MOE_SUB, issue, 0, unroll=8)
        pltpu.make_async_copy(src_hbm.at[pl.ds(0, MOE_SUB), :], o_ref, sem).wait()


def moe_dispatch(src_tok, tile_start, st_nsub, st_out_block, h2_packed, m_pad):
    width = h2_packed.shape[1]
    grid_spec = pltpu.PrefetchScalarGridSpec(
        num_scalar_prefetch=4,
        grid=(st_nsub.shape[0], MOE_SUBS_PER_TILE),
        in_specs=[pl.BlockSpec(memory_space=pl.ANY)],
        out_specs=pl.BlockSpec((MOE_SUB, width),
                               lambda s, i, tok, start, ns, ob: (ob[s * MOE_SUBS_PER_TILE + i], 0)),
        scratch_shapes=[pltpu.SemaphoreType.DMA(())],
    )
    return pl.pallas_call(
        _dispatch_kernel,
        grid_spec=grid_spec,
        out_shape=jax.ShapeDtypeStruct((m_pad, width), h2_packed.dtype),
        compiler_params=_cparams("arbitrary", "arbitrary"),
        name="moe_dispatch",
    )(src_tok, tile_start, st_nsub, st_out_block, h2_packed)


def _unpack_bf16_pairs(words):
    lo = pltpu.bitcast(words << 16, F32).astype(BF16)
    hi = pltpu.bitcast(words & jnp.uint32(0xFFFF0000), F32).astype(BF16)
    return lo, hi


def _moe_up_kernel(e_ref, blk_ref, ns_ref, x_ref, wg_ref, wu_ref, bg_ref, bu_ref, o_ref,
                   xlo, xhi, wg_bf, wu_bf):
    s = pl.program_id(0)
    nsub = ns_ref[s]
    half = xlo.shape[1]

    @pl.when((pl.program_id(1) == 0) & (nsub > 0))
    def _():
        for i in range(MOE_SUBS_PER_TILE):
            @pl.when(i < nsub)
            def _():
                rows = slice(i * MOE_SUB, (i + 1) * MOE_SUB)
                lo, hi = _unpack_bf16_pairs(x_ref[rows, :])
                xlo[rows, :] = lo
                xhi[rows, :] = hi

    def compute(i):
        rows = slice(i * MOE_SUB, (i + 1) * MOE_SUB)
        lo = xlo[rows, :]
        hi = xhi[rows, :]
        g = (jnp.dot(lo, wg_bf[:half, :], preferred_element_type=F32)
             + jnp.dot(hi, wg_bf[half:, :], preferred_element_type=F32) + bg_ref[...])
        u = (jnp.dot(lo, wu_bf[:half, :], preferred_element_type=F32)
             + jnp.dot(hi, wu_bf[half:, :], preferred_element_type=F32) + bu_ref[...])
        g = jnp.minimum(g, SWIGLU_LIMIT)
        u = jnp.clip(u, -SWIGLU_LIMIT, SWIGLU_LIMIT)
        o_ref[rows, :] = ((u + 1.0) * g * jax.nn.sigmoid(SWIGLU_ALPHA * g)).astype(o_ref.dtype)

    @pl.when(nsub == MOE_SUBS_PER_TILE)
    def _():
        _cast_rows(wg_ref, wg_bf)
        _cast_rows(wu_ref, wu_bf)
        for i in range(MOE_SUBS_PER_TILE):
            compute(i)

    @pl.when((nsub > 0) & (nsub < MOE_SUBS_PER_TILE))
    def _():
        _cast_rows(wg_ref, wg_bf)
        _cast_rows(wu_ref, wu_bf)
        for i in range(MOE_SUBS_PER_TILE):
            @pl.when(i < nsub)
            def _():
                compute(i)

            @pl.when(i >= nsub)
            def _():
                rows = slice(i * MOE_SUB, (i + 1) * MOE_SUB)
                o_ref[rows, :] = jnp.zeros((MOE_SUB, o_ref.shape[1]), o_ref.dtype)


def _frozen_col(n, ns, s, last):
    return jnp.where(ns[s] > 0, n, last)


def moe_up(st_expert, st_block, st_nsub, xs, w_gate_up, b_gate_up):
    m_pad, half = xs.shape
    d = 2 * half
    n_steps = st_expert.shape[0]
    nb = D_FF // FF_TN
    last = nb - 1
    grid_spec = pltpu.PrefetchScalarGridSpec(
        num_scalar_prefetch=3,
        grid=(n_steps, nb),
        in_specs=[
            pl.BlockSpec((MOE_TILE, half), lambda s, n, e, blk, ns: (blk[s], 0)),
            pl.BlockSpec((None, d, FF_TN), lambda s, n, e, blk, ns: (e[s], 0, _frozen_col(n, ns, s, last))),
            pl.BlockSpec((None, d, FF_TN), lambda s, n, e, blk, ns: (e[s], 0, nb + _frozen_col(n, ns, s, last))),
            pl.BlockSpec((None, 1, FF_TN), lambda s, n, e, blk, ns: (e[s], 0, _frozen_col(n, ns, s, last))),
            pl.BlockSpec((None, 1, FF_TN), lambda s, n, e, blk, ns: (e[s], 0, nb + _frozen_col(n, ns, s, last))),
        ],
        out_specs=pl.BlockSpec((MOE_TILE, FF_TN), lambda s, n, e, blk, ns: (blk[s], _frozen_col(n, ns, s, last))),
        scratch_shapes=[
            pltpu.VMEM((MOE_TILE, half), BF16),
            pltpu.VMEM((MOE_TILE, half), BF16),
            pltpu.VMEM((d, FF_TN), BF16),
            pltpu.VMEM((d, FF_TN), BF16),
        ],
    )
    return pl.pallas_call(
        _moe_up_kernel,
        grid_spec=grid_spec,
        out_shape=jax.ShapeDtypeStruct((m_pad, D_FF), BF16),
        compiler_params=_cparams("arbitrary", "arbitrary"),
        name="moe_up",
    )(st_expert, st_block, st_nsub, xs, w_gate_up, w_gate_up, b_gate_up, b_gate_up)


def _moe_down_kernel(e_ref, blk_ref, ns_ref, x_ref, w_ref, b_ref, o_ref, w_bf):
    nsub = ns_ref[pl.program_id(0)]

    def compute(i):
        rows = slice(i * MOE_SUB, (i + 1) * MOE_SUB)
        o_ref[rows, :] = jnp.dot(x_ref[rows, :], w_bf[...], preferred_element_type=F32) + b_ref[...]

    @pl.when(nsub == MOE_SUBS_PER_TILE)
    def _():
        _cast_rows(w_ref, w_bf)
        for i in range(MOE_SUBS_PER_TILE):
            compute(i)

    @pl.when((nsub > 0) & (nsub < MOE_SUBS_PER_TILE))
    def _():
        _cast_rows(w_ref, w_bf)
        for i in range(MOE_SUBS_PER_TILE):
            @pl.when(i < nsub)
            def _():
                compute(i)

            @pl.when(i >= nsub)
            def _():
                rows = slice(i * MOE_SUB, (i + 1) * MOE_SUB)
                o_ref[rows, :] = jnp.zeros((MOE_SUB, o_ref.shape[1]), o_ref.dtype)


def moe_down(st_expert, st_block, st_nsub, act, w_down, b_down):
    m_pad, dff = act.shape
    d = w_down.shape[2]
    n_steps = st_expert.shape[0]
    nb = d // DOWN_TN
    last = nb - 1
    grid_spec = pltpu.PrefetchScalarGridSpec(
        num_scalar_prefetch=3,
        grid=(n_steps, nb),
        in_specs=[
            pl.BlockSpec((MOE_TILE, dff), lambda s, n, e, blk, ns: (blk[s], 0)),
            pl.BlockSpec((None, dff, DOWN_TN), lambda s, n, e, blk, ns: (e[s], 0, _frozen_col(n, ns, s, last))),
            pl.BlockSpec((None, 1, DOWN_TN), lambda s, n, e, blk, ns: (e[s], 0, _frozen_col(n, ns, s, last))),
        ],
        out_specs=pl.BlockSpec((MOE_TILE, DOWN_TN), lambda s, n, e, blk, ns: (blk[s], _frozen_col(n, ns, s, last))),
        scratch_shapes=[pltpu.VMEM((dff, DOWN_TN), BF16)],
    )
    return pl.pallas_call(
        _moe_down_kernel,
        grid_spec=grid_spec,
        out_shape=jax.ShapeDtypeStruct((m_pad, d), F32),
        compiler_params=_cparams("arbitrary", "arbitrary"),
        name="moe_down",
    )(st_expert, st_block, st_nsub, act, w_down, b_down)


def _combine_kernel(pos_ref, ys_hbm, x1_ref, g2_ref, tw_ref, o_ref, buf, sems):
    i = pl.program_id(0)
    n_steps = pl.num_programs(0)
    t = COMBINE_TOKENS

    def row_copy(slot, k, r, p):
        return pltpu.make_async_copy(ys_hbm.at[pl.ds(p, 1), :], buf.at[slot, k, pl.ds(r, 1), :], sems.at[slot])

    def start(step, slot):
        def issue(r, _):
            for k in range(TOP_K):
                row_copy(slot, k, r, pos_ref[(step * t + r) * TOP_K + k]).start()
            return 0

        lax.fori_loop(0, t, issue, 0)

    @pl.when(i == 0)
    def _():
        start(0, 0)

    @pl.when(i + 1 < n_steps)
    def _():
        start(i + 1, (i + 1) % 2)

    slot = i % 2

    def wait(r, _):
        for k in range(TOP_K):
            row_copy(slot, k, r, 0).wait()
        return 0

    lax.fori_loop(0, t, wait, 0)
    tw = tw_ref[...]
    y = tw[:, 0:1] * buf[slot, 0]
    for k in range(1, TOP_K):
        y = y + tw[:, k:k + 1] * buf[slot, k]
    d = y.shape[1]
    y3 = y.reshape(t // SUBLANES, SUBLANES, d)
    o_ref[...] = x1_ref[...] + g2_ref[...] * y3


def moe_combine(pos, ys, x1_3, mod4, gate_chunk, top_w, prompt_groups, groups_per_seq):
    g, r, d = x1_3.shape
    gb = COMBINE_TOKENS // r
    grid_spec = pltpu.PrefetchScalarGridSpec(
        num_scalar_prefetch=1,
        grid=(g // gb,),
        in_specs=[
            pl.BlockSpec(memory_space=pl.ANY),
            pl.BlockSpec((gb, r, d), lambda i, pos: (i, 0, 0)),
            _mod_spec(gate_chunk, gb, prompt_groups, groups_per_seq, d),
            pl.BlockSpec((COMBINE_TOKENS, ROUTER_LANES), lambda i, pos: (i, 0)),
        ],
        out_specs=pl.BlockSpec((gb, r, d), lambda i, pos: (i, 0, 0)),
        scratch_shapes=[
            pltpu.VMEM((2, TOP_K, COMBINE_TOKENS, d), F32),
            pltpu.SemaphoreType.DMA((2,)),
        ],
    )
    return pl.pallas_call(
        _combine_kernel,
        grid_spec=grid_spec,
        out_shape=jax.ShapeDtypeStruct((g, r, d), F32),
        compiler_params=_cparams("arbitrary"),
        name="moe_combine",
    )(pos, ys, x1_3, mod4, top_w)


def _routing_tables(top_i, n_tiles):
    n = top_i.shape[0]
    m = n * TOP_K
    flat_e = top_i.reshape(m)
    order = jnp.argsort(flat_e, stable=True).astype(I32)
    e_sorted = flat_e[order]
    counts = jnp.sum(jax.nn.one_hot(flat_e, N_EXPERTS, dtype=I32), axis=0)
    starts = jnp.cumsum(counts) - counts
    tiles_per_e = (counts + MOE_TILE - 1) // MOE_TILE
    tile_end = jnp.cumsum(tiles_per_e)
    tile_start = tile_end - tiles_per_e
    total_tiles = tile_end[-1]
    rank = jnp.arange(m, dtype=I32) - starts[e_sorted]
    dest = tile_start[e_sorted] * MOE_TILE + rank
    pos = jnp.zeros((m,), I32).at[order].set(dest)
    src_tok = order // TOP_K
    s = jnp.arange(n_tiles, dtype=I32)
    s_eff = jnp.minimum(s, total_tiles - 1)
    st_expert = jnp.searchsorted(tile_end, s_eff, side="right").astype(I32)
    within = (s_eff - tile_start[st_expert]) * MOE_TILE
    rows_valid = jnp.where(s < total_tiles, jnp.clip(counts[st_expert] - within, 0, MOE_TILE), 0)
    st_nsub = ((rows_valid + MOE_SUB - 1) // MOE_SUB).astype(I32)
    sorted_start = (starts[st_expert] + within).astype(I32)
    sub = jnp.arange(MOE_SUBS_PER_TILE, dtype=I32)[None, :]
    own = jnp.where(sub < st_nsub[:, None], s[:, None] * MOE_SUBS_PER_TILE + sub, -1).reshape(-1)
    st_out_block = lax.cummax(own, axis=0).astype(I32)
    return pos, src_tok.astype(I32), sorted_start, st_expert, s_eff.astype(I32), st_nsub, st_out_block


def _rope_tables(pos):
    half = RET_DK // 2
    inv = ROPE_BASE ** (-jnp.arange(half, dtype=F32) / half)
    ang = pos.astype(F32)[:, None] * inv[None, :]
    return jnp.cos(ang), jnp.sin(ang)


def kernel(x_prompt, x_sample, c_prompt, c_sample, cache_k, cache_v, cache_logf, state_ret, page_table, rms1_g, rms2_g, w_ada, b_ada, w_in, b_forget, q_norm_g, k_norm_g, ret_norm_g, w_branch_fox, w_branch_ret, w_out, w_router, b_router, w_gate_up, b_gate_up, w_down, b_down):
    batch, seq, d = x_prompt.shape
    bd, dec_t, _ = x_sample.shape
    depth = w_in.shape[0]
    assert depth == 1 and d == D_MODEL and dec_t == SUBLANES
    n_p = batch * seq
    n_s = bd * dec_t
    n = n_p + n_s
    groups = n // SUBLANES
    past_len = page_table.shape[1] * cache_k.shape[2]

    c_all = jnp.concatenate([jnp.repeat(c_prompt, MOD_REPEAT, axis=0), c_sample], axis=0)
    mod = ada_modulation(c_all, w_ada[0], b_ada[0].reshape(1, -1))
    mod4 = jnp.transpose(mod.reshape(c_all.shape[0], N_MOD, 1, d), (1, 0, 2, 3))
    shift1, scale1, gate1, shift2, scale2, gate2 = range(N_MOD)
    prompt_groups = n_p // SUBLANES
    groups_per_seq = seq // SUBLANES

    x_all = jnp.concatenate([x_prompt.reshape(n_p, d), x_sample.reshape(n_s, d)], axis=0)
    x3 = x_all.reshape(groups, SUBLANES, d)
    h = norm_modulate(x3, mod4, shift1, scale1, rms1_g[0], prompt_groups, groups_per_seq)

    pos = jnp.concatenate([jnp.tile(jnp.arange(seq, dtype=I32), batch),
                           jnp.tile(past_len + jnp.arange(dec_t, dtype=I32), bd)])
    cos, sin = _rope_tables(pos)
    w_in_t = jnp.swapaxes(w_in[0], 0, 1)
    b_forget_pad = jnp.zeros((1, LANES), F32).at[0, :FOX_HEADS].set(b_forget[0])
    q, k32, kbf, v32, vbf, logf, qr, kr, vr, gates = in_projection(
        h, w_in_t, b_forget_pad, q_norm_g[0].reshape(1, -1), k_norm_g[0].reshape(1, -1), cos, sin)

    c_cum, c_cum_t = cumsum_log_forget(logf, batch, seq)
    o_fox_p = fox_prompt_attention(q, kbf, vbf, c_cum, c_cum_t, batch, seq)
    cache_logf_t = jnp.swapaxes(cache_logf[0], 1, 2)
    logf_s = logf[n_p:, :FOX_HEADS].reshape(bd, dec_t, FOX_HEADS)
    logf_new_t = jnp.zeros((bd, FOX_HEADS, LANES), F32).at[:, :, :dec_t].set(jnp.swapaxes(logf_s, 1, 2))
    o_fox_s = fox_sample_attention(page_table, q[n_p:].astype(F32), cache_k[0], cache_v[0], cache_logf_t,
                                   k32[n_p:], v32[n_p:], logf_new_t)
    o_fox = jnp.concatenate([o_fox_p, o_fox_s.astype(BF16)], axis=0)

    g_ret = ret_norm_g[0].reshape(1, -1)
    o_ret_p, state_p = retention_prompt(qr, kr, vr, gates, g_ret,
                                        _retention_consts(RET_CHUNK, RET_CHUNK, RET_CHUNK), batch, seq)
    o_ret_s, state_s = retention_sample(qr[n_p:].astype(F32), kr[n_p:].astype(F32), vr[n_p:].astype(F32),
                                        gates[n_p:, :RET_WIDTH], g_ret, state_ret[0],
                                        _retention_consts(dec_t, RET_SAMPLE_Q_ROWS, RET_CHUNK))
    o_ret = jnp.concatenate([o_ret_p, o_ret_s.astype(BF16)], axis=0)

    gate_blocks = RET_WIDTH // TN
    (mixed_in,) = matmul_ws(
        [o_fox, o_ret], [w_branch_fox[0], w_branch_ret[0]], col_offsets=[0, 0], n_blocks=d // TN,
        epilogue=_ep_branch_merge, extra=(gates, gates),
        extra_specs=(_tile_spec(off=gate_blocks), _tile_spec(off=gate_blocks + d // TN)),
        out_shapes=[jax.ShapeDtypeStruct((n, d), BF16)], out_specs=[_tile_spec()], name="branch_merge")
    (mixed,) = matmul_ws([mixed_in], [w_out[0]], col_offsets=[0], n_blocks=d // TN, epilogue=_ep_plain,
                         out_shapes=[jax.ShapeDtypeStruct((n, d), F32)], out_specs=[_tile_spec()], name="out_proj")

    w_router_pad = jnp.zeros((d, ROUTER_LANES), F32).at[:, :N_EXPERTS].set(w_router[0])
    b_router_pad = jnp.zeros((1, ROUTER_LANES), F32).at[0, :N_EXPERTS].set(b_router[0])
    x1, h2_packed, top_w, top_i = resid_norm_router(
        x3, mixed.reshape(groups, SUBLANES, d), mod4, gate1, shift2, scale2, rms2_g[0], w_router_pad, b_router_pad,
        prompt_groups, groups_per_seq)

    n_tiles = N_EXPERTS + (n * TOP_K) // MOE_TILE
    pos_rows, src_tok, sorted_start, st_expert, st_block, st_nsub, st_out_block = _routing_tables(
        top_i[:, :TOP_K], n_tiles)
    xs = moe_dispatch(src_tok, sorted_start, st_nsub, st_out_block, h2_packed, n_tiles * MOE_TILE)
    act = moe_up(st_expert, st_block, st_nsub, xs, w_gate_up[0], b_gate_up[0].reshape(N_EXPERTS, 1, -1))
    ys = moe_down(st_expert, st_block, st_nsub, act, w_down[0], b_down[0].reshape(N_EXPERTS, 1, -1))
    y = moe_combine(pos_rows, ys, x1, mod4, gate2, top_w, prompt_groups, groups_per_seq).reshape(n, d)

    y_prompt = y[:n_p].reshape(batch, seq, d)
    y_sample = y[n_p:].reshape(bd, dec_t, d)
    kv_p = (depth, batch, seq, FOX_HEADS, FOX_HEAD_DIM)
    kv_s = (depth, bd, dec_t, FOX_HEADS, FOX_HEAD_DIM)
    logf16 = logf[:, :FOX_HEADS]
    return (y_prompt, y_sample,
            k32[:n_p].reshape(kv_p), v32[:n_p].reshape(kv_p), logf16[:n_p].reshape(depth, batch, seq, FOX_HEADS),
            state_p[None],
            k32[n_p:].reshape(kv_s), v32[n_p:].reshape(kv_s), logf16[n_p:].reshape(depth, bd, dec_t, FOX_HEADS),
            state_s[None])
```

```python
import functools

import jax
import jax.numpy as jnp
import numpy as np
from jax import lax
from jax.experimental import pallas as pl
from jax.experimental.pallas import tpu as pltpu

F32 = jnp.float32
BF16 = jnp.bfloat16
I32 = jnp.int32

D_MODEL = 4096
FOX_HEADS = 16
FOX_HEAD_DIM = 128
FOX_WIDTH = FOX_HEADS * FOX_HEAD_DIM
RET_HEADS = 8
RET_DK = 256
RET_DV = 256
RET_WIDTH = RET_HEADS * RET_DV
RET_CHUNK = 128
ROPE_BASE = 10000.0
N_EXPERTS = 32
TOP_K = 4
D_FF = D_MODEL
SWIGLU_LIMIT = 7.0
SWIGLU_ALPHA = 1.702
EPS = 1e-6
N_MOD = 6
PAGE_SIZE = 128

SUBLANES = 8
LANES = 128
VMEM_LIMIT_BYTES = 56 * 1024 * 1024

TM = 1024
TN = 512
CAST_ROWS = 512


def _cparams(*sem):
    return pltpu.CompilerParams(dimension_semantics=sem, vmem_limit_bytes=VMEM_LIMIT_BYTES)


def _cast_rows(src_ref, dst_ref):
    rows = src_ref.shape[0]
    step = min(CAST_ROWS, rows)

    def body(i, _):
        r = pl.multiple_of(i * step, step)
        dst_ref[pl.ds(r, step), :] = src_ref[pl.ds(r, step), :].astype(BF16)
        return 0

    lax.fori_loop(0, rows // step, body, 0)


def _ada_kernel(c_ref, w_ref, b_ref, o_ref, wbf):
    _cast_rows(w_ref, wbf)
    c = c_ref[...]
    a = (c * jax.nn.sigmoid(c)).astype(BF16)
    o_ref[...] = jnp.dot(a, wbf[...], preferred_element_type=F32) + b_ref[...]


def ada_modulation(c_all, w_ada, b_ada):
    rows, d = c_all.shape
    n = w_ada.shape[1]
    return pl.pallas_call(
        _ada_kernel,
        grid=(n // TN,),
        in_specs=[
            pl.BlockSpec((rows, d), lambda j: (0, 0)),
            pl.BlockSpec((d, TN), lambda j: (0, j)),
            pl.BlockSpec((1, TN), lambda j: (0, j)),
        ],
        out_specs=pl.BlockSpec((rows, TN), lambda j: (0, j)),
        out_shape=jax.ShapeDtypeStruct((rows, n), F32),
        scratch_shapes=[pltpu.VMEM((d, TN), BF16)],
        compiler_params=_cparams("arbitrary"),
        name="ada_modulation",
    )(c_all, w_ada, b_ada)


NORM_GROUPS = 16
MOD_REPEAT = 16


def _mod_spec(chunk, gb, prompt_groups, groups_per_seq, d):
    assert MOD_REPEAT % gb == 0 and groups_per_seq % gb == 0 and prompt_groups % gb == 0
    n_prompt_rows = (prompt_groups // groups_per_seq) * MOD_REPEAT

    def index(i, *_):
        g0 = i * gb
        prompt_blk = (g0 // groups_per_seq) * (MOD_REPEAT // gb)
        sample_blk = (n_prompt_rows + g0 - prompt_groups) // gb
        return (chunk, jnp.where(g0 < prompt_groups, prompt_blk, sample_blk), 0, 0)

    return pl.BlockSpec((None, gb, 1, d), index)


def _norm_mod_kernel(x_ref, sh_ref, sc_ref, g_ref, o_ref):
    x = x_ref[...]
    ms = jnp.mean(x * x, axis=-1, keepdims=True)
    y = x * lax.rsqrt(ms + EPS) * g_ref[...]
    h = y * (1.0 + sc_ref[...]) + sh_ref[...]
    o_ref[...] = h.reshape(o_ref.shape).astype(o_ref.dtype)


def norm_modulate(x3, mod4, shift_chunk, scale_chunk, gain, prompt_groups, groups_per_seq):
    g, r, d = x3.shape
    gb = NORM_GROUPS
    mod_spec = functools.partial(_mod_spec, gb=gb, prompt_groups=prompt_groups, groups_per_seq=groups_per_seq, d=d)
    return pl.pallas_call(
        _norm_mod_kernel,
        grid=(g // gb,),
        in_specs=[
            pl.BlockSpec((gb, r, d), lambda i: (i, 0, 0)),
            mod_spec(shift_chunk),
            mod_spec(scale_chunk),
            pl.BlockSpec((1, 1, d), lambda i: (0, 0, 0)),
        ],
        out_specs=pl.BlockSpec((gb * r, d), lambda i: (i, 0)),
        out_shape=jax.ShapeDtypeStruct((g * r, d), BF16),
        compiler_params=_cparams("arbitrary"),
        name="norm_modulate",
    )(x3, mod4, mod4, gain.reshape(1, 1, d))


def _cast_rows_transposed(src_ref, dst_ref):
    k = src_ref.shape[1]
    for c in range(k // CAST_ROWS):
        rows = slice(c * CAST_ROWS, (c + 1) * CAST_ROWS)
        dst_ref[rows, :] = src_ref[:, rows].T.astype(BF16)


def _mm_kernel(*refs, n_a, n_extra, n_out, epilogue, transposed):
    a_refs = refs[:n_a]
    w_refs = refs[n_a:2 * n_a]
    extra = refs[2 * n_a:2 * n_a + n_extra]
    outs = refs[2 * n_a + n_extra:2 * n_a + n_extra + n_out]
    wbfs = refs[2 * n_a + n_extra + n_out:]

    @pl.when(pl.program_id(1) == 0)
    def _():
        for w_ref, wbf in zip(w_refs, wbfs):
            (_cast_rows_transposed if transposed else _cast_rows)(w_ref, wbf)

    accs = [jnp.dot(a[...], wbf[...], preferred_element_type=F32) for a, wbf in zip(a_refs, wbfs)]
    epilogue(accs, extra, outs)


def matmul_ws(a_list, w_list, *, col_offsets, n_blocks, epilogue, out_shapes, out_specs,
              extra=(), extra_specs=(), tn=TN, tm=TM, transposed=False, name):
    m_rows = a_list[0].shape[0]
    in_specs = []
    for a in a_list:
        in_specs.append(pl.BlockSpec((tm, a.shape[1]), lambda n, m: (m, 0)))
    for w, off in zip(w_list, col_offsets):
        if transposed:
            assert off % SUBLANES == 0 and tn % SUBLANES == 0
            in_specs.append(pl.BlockSpec(
                (pl.Element(tn), pl.Element(w.shape[1])),
                functools.partial(lambda n, m, off: (pl.multiple_of(off + n * tn, SUBLANES), 0), off=off)))
        else:
            in_specs.append(pl.BlockSpec((w.shape[0], tn),
                                         functools.partial(lambda n, m, off: (0, n + off), off=off)))
    in_specs += list(extra_specs)
    kernel = functools.partial(_mm_kernel, n_a=len(a_list), n_extra=len(extra), n_out=len(out_shapes),
                               epilogue=epilogue, transposed=transposed)
    k_dims = [w.shape[1] if transposed else w.shape[0] for w in w_list]
    return pl.pallas_call(
        kernel,
        grid=(n_blocks, m_rows // tm),
        in_specs=in_specs,
        out_specs=out_specs,
        out_shape=out_shapes,
        scratch_shapes=[pltpu.VMEM((k, tn), BF16) for k in k_dims],
        compiler_params=_cparams("arbitrary", "arbitrary"),
        name=name,
    )(*a_list, *w_list, *extra)


def _tile_spec(tm=TM, tn=TN, off=0):
    return pl.BlockSpec((tm, tn), functools.partial(lambda n, m, off: (m, n + off), off=off))


def _ep_plain(accs, extra, outs):
    for o in outs:
        o[...] = accs[0].astype(o.dtype)


def _ep_head_norm(accs, extra, outs, *, mult):
    acc = accs[0]
    g = extra[0][...]
    for j in range(acc.shape[1] // FOX_HEAD_DIM):
        z = acc[:, j * FOX_HEAD_DIM:(j + 1) * FOX_HEAD_DIM]
        ms = jnp.mean(z * z, axis=-1, keepdims=True)
        y = z * lax.rsqrt(ms + EPS) * g
        for o, s in zip(outs, mult):
            o[:, j * FOX_HEAD_DIM:(j + 1) * FOX_HEAD_DIM] = (y if s == 1.0 else y * s).astype(o.dtype)


def _ep_log_forget(accs, extra, outs):
    x = accs[0] + extra[0][...]
    lf = jnp.minimum(x, 0.0) - jnp.log1p(jnp.exp(-jnp.abs(x)))
    lane = lax.broadcasted_iota(I32, lf.shape, 1)
    outs[0][...] = jnp.where(lane < FOX_HEADS, lf, 0.0)


def _ep_rope(accs, extra, outs, *, mult):
    acc = accs[0]
    cos = extra[0][...]
    sin = extra[1][...]
    half = RET_DK // 2
    for j in range(acc.shape[1] // RET_DK):
        x1 = acc[:, j * RET_DK:j * RET_DK + half]
        x2 = acc[:, j * RET_DK + half:(j + 1) * RET_DK]
        y1 = x1 * cos - x2 * sin
        y2 = x2 * cos + x1 * sin
        if mult != 1.0:
            y1 = y1 * mult
            y2 = y2 * mult
        outs[0][:, j * RET_DK:j * RET_DK + half] = y1.astype(outs[0].dtype)
        outs[0][:, j * RET_DK + half:(j + 1) * RET_DK] = y2.astype(outs[0].dtype)


def _ep_branch_merge(accs, extra, outs):
    gf = jax.nn.sigmoid(extra[0][...])
    gr = jax.nn.sigmoid(extra[1][...])
    outs[0][...] = (gf * accs[0] + gr * accs[1]).astype(outs[0].dtype)


IN_FL_START = 3 * FOX_WIDTH
IN_TAIL_START = IN_FL_START + FOX_HEADS
GATE_COLS = RET_WIDTH + 2 * D_MODEL


def in_projection(h, w_in_t, b_forget_pad, q_norm_g, k_norm_g, cos, sin):
    n = h.shape[0]
    sds = jax.ShapeDtypeStruct
    wide = FOX_WIDTH // TN
    g_spec = pl.BlockSpec((1, FOX_HEAD_DIM), lambda j, m: (0, 0))
    rope_specs = [pl.BlockSpec((TM, RET_DK // 2), lambda j, m: (m, 0))] * 2
    mm = functools.partial(matmul_ws, [h], [w_in_t], transposed=True)

    (q,) = mm(col_offsets=[0], n_blocks=wide,
              epilogue=functools.partial(_ep_head_norm, mult=(FOX_HEAD_DIM ** -0.5,)),
              extra=(q_norm_g,), extra_specs=(g_spec,),
              out_shapes=[sds((n, FOX_WIDTH), BF16)], out_specs=[_tile_spec()], name="in_proj_fox_q")
    k32, kbf = mm(col_offsets=[FOX_WIDTH], n_blocks=wide,
                  epilogue=functools.partial(_ep_head_norm, mult=(1.0, 1.0)),
                  extra=(k_norm_g,), extra_specs=(g_spec,),
                  out_shapes=[sds((n, FOX_WIDTH), F32), sds((n, FOX_WIDTH), BF16)],
                  out_specs=[_tile_spec(), _tile_spec()], name="in_proj_fox_k")
    v32, vbf = mm(col_offsets=[2 * FOX_WIDTH], n_blocks=wide, epilogue=_ep_plain,
                  out_shapes=[sds((n, FOX_WIDTH), F32), sds((n, FOX_WIDTH), BF16)],
                  out_specs=[_tile_spec(), _tile_spec()], name="in_proj_fox_v")
    (logf,) = mm(col_offsets=[IN_FL_START], n_blocks=1, tn=LANES,
                 epilogue=_ep_log_forget, extra=(b_forget_pad,),
                 extra_specs=(pl.BlockSpec((1, LANES), lambda j, m: (0, 0)),),
                 out_shapes=[sds((n, LANES), F32)], out_specs=[_tile_spec(tn=LANES)],
                 name="in_proj_log_forget")
    (qr,) = mm(col_offsets=[IN_TAIL_START], n_blocks=wide,
               epilogue=functools.partial(_ep_rope, mult=1.0), extra=(cos, sin), extra_specs=rope_specs,
               out_shapes=[sds((n, RET_WIDTH), BF16)], out_specs=[_tile_spec()], name="in_proj_ret_q")
    (kr,) = mm(col_offsets=[IN_TAIL_START + RET_WIDTH], n_blocks=wide,
               epilogue=functools.partial(_ep_rope, mult=RET_DK ** -0.5), extra=(cos, sin),
               extra_specs=rope_specs,
               out_shapes=[sds((n, RET_WIDTH), BF16)], out_specs=[_tile_spec()], name="in_proj_ret_k")
    (vr,) = mm(col_offsets=[IN_TAIL_START + 2 * RET_WIDTH], n_blocks=wide, epilogue=_ep_plain,
               out_shapes=[sds((n, RET_WIDTH), BF16)], out_specs=[_tile_spec()], name="in_proj_ret_v")
    (gates,) = mm(col_offsets=[IN_TAIL_START + 3 * RET_WIDTH], n_blocks=GATE_COLS // TN, epilogue=_ep_plain,
                  out_shapes=[sds((n, GATE_COLS), F32)], out_specs=[_tile_spec()], name="in_proj_gates")
    return q, k32, kbf, v32, vbf, logf, qr, kr, vr, gates


CUMSUM_BLOCK = 256
HIGHEST = lax.Precision.HIGHEST


def _cumsum_kernel(lf_ref, c_ref, ct_ref, carry):
    @pl.when(pl.program_id(1) == 0)
    def _():
        carry[...] = jnp.zeros_like(carry)

    blk = lf_ref.shape[0]
    r = lax.broadcasted_iota(I32, (blk, blk), 0)
    c = lax.broadcasted_iota(I32, (blk, blk), 1)
    lower = (c <= r).astype(F32)
    cs = jnp.dot(lower, lf_ref[...], precision=HIGHEST, preferred_element_type=F32) + carry[0:1, :]
    c_ref[...] = cs
    er = lax.broadcasted_iota(I32, (FOX_HEADS, LANES), 0)
    ec = lax.broadcasted_iota(I32, (FOX_HEADS, LANES), 1)
    eye = (er == ec).astype(F32)
    ct_ref[0] = lax.dot_general(eye, cs, (((1,), (1,)), ((), ())), precision=HIGHEST,
                                preferred_element_type=F32)
    carry[...] = jnp.broadcast_to(cs[blk - 1:blk, :], carry.shape)


def cumsum_log_forget(logf_pad, batch, seq):
    nb = seq // CUMSUM_BLOCK
    return pl.pallas_call(
        _cumsum_kernel,
        grid=(batch, nb),
        in_specs=[pl.BlockSpec((CUMSUM_BLOCK, LANES), lambda b, s: (b * nb + s, 0))],
        out_specs=[
            pl.BlockSpec((CUMSUM_BLOCK, LANES), lambda b, s: (b * nb + s, 0)),
            pl.BlockSpec((1, FOX_HEADS, CUMSUM_BLOCK), lambda b, s: (b, 0, s)),
        ],
        out_shape=[
            jax.ShapeDtypeStruct((batch * seq, LANES), F32),
            jax.ShapeDtypeStruct((batch, FOX_HEADS, seq), F32),
        ],
        scratch_shapes=[pltpu.VMEM((SUBLANES, LANES), F32)],
        compiler_params=_cparams("arbitrary", "arbitrary"),
        name="cumsum_log_forget",
    )(logf_pad)


ATT_BLOCK = 512


def _softmax_update(s, v, m, l, acc):
    m_new = jnp.maximum(m, jnp.max(s, axis=-1, keepdims=True))
    alpha = jnp.exp(m - m_new)
    p = jnp.exp(s - m_new)
    l_new = alpha * l + jnp.sum(p, axis=-1, keepdims=True)
    acc_new = alpha * acc + jnp.dot(p.astype(BF16), v, preferred_element_type=F32)
    return m_new, l_new, acc_new


def _fox_prompt_kernel(q_ref, k_ref, v_ref, c_ref, ct_ref, o_ref):
    h = pl.program_id(1)
    qi = pl.program_id(2)
    t = ATT_BLOCK
    q = q_ref[...]
    lane = lax.broadcasted_iota(I32, (t, LANES), 1)
    cq = jnp.sum(jnp.where(lane == h, c_ref[...], 0.0), axis=1, keepdims=True)
    sub = lax.broadcasted_iota(I32, (FOX_HEADS, t), 0)

    def scores(j):
        start = pl.multiple_of(j * t, t)
        k = k_ref[pl.ds(start, t), :]
        s = lax.dot_general(q, k, (((1,), (1,)), ((), ())), preferred_element_type=F32)
        ck = jnp.sum(jnp.where(sub == h, ct_ref[0, :, pl.ds(start, t)], 0.0), axis=0, keepdims=True)
        return s + (cq - ck), v_ref[pl.ds(start, t), :]

    def body(j, carry):
        s, v = scores(j)
        return _softmax_update(s, v, *carry)

    init = (jnp.full((t, 1), -jnp.inf, F32), jnp.zeros((t, 1), F32), jnp.zeros((t, FOX_HEAD_DIM), F32))
    carry = lax.fori_loop(0, qi, body, init)
    s, v = scores(qi)
    row = lax.broadcasted_iota(I32, (t, t), 0)
    col = lax.broadcasted_iota(I32, (t, t), 1)
    s = jnp.where(col <= row, s, -jnp.inf)
    _, l, acc = _softmax_update(s, v, *carry)
    o_ref[...] = (acc / l).astype(o_ref.dtype)


def fox_prompt_attention(q, k, v, c, ct, batch, seq):
    nq = seq // ATT_BLOCK
    hd = FOX_HEAD_DIM
    return pl.pallas_call(
        _fox_prompt_kernel,
        grid=(batch, FOX_HEADS, nq),
        in_specs=[
            pl.BlockSpec((ATT_BLOCK, hd), lambda b, h, i: (b * nq + i, h)),
            pl.BlockSpec((seq, hd), lambda b, h, i: (b, h)),
            pl.BlockSpec((seq, hd), lambda b, h, i: (b, h)),
            pl.BlockSpec((ATT_BLOCK, LANES), lambda b, h, i: (b * nq + i, 0)),
            pl.BlockSpec((1, FOX_HEADS, seq), lambda b, h, i: (b, 0, 0)),
        ],
        out_specs=pl.BlockSpec((ATT_BLOCK, hd), lambda b, h, i: (b * nq + i, h)),
        out_shape=jax.ShapeDtypeStruct((batch * seq, FOX_WIDTH), BF16),
        compiler_params=_cparams("arbitrary", "arbitrary", "arbitrary"),
        name="fox_prompt_attention",
    )(q, k, v, c, ct)


FOX_PAGES_PER_STEP = 8
def _expand_heads(x):
    rows = [jnp.broadcast_to(x[h:h + 1, :], (SUBLANES, x.shape[1])) for h in range(FOX_HEADS)]
    return jnp.concatenate(rows, axis=0)


def _fox_sample_kernel(pt_ref, q_ref, *rest, n_steps):
    pp = FOX_PAGES_PER_STEP
    kc_refs = rest[:pp]
    vc_refs = rest[pp:2 * pp]
    lft_refs = rest[2 * pp:3 * pp]
    kn_ref, vn_ref, lfnt_ref, o_ref, qblk, hmask, m_sc, l_sc, acc_sc, carry_sc, cq_sc, cnk_sc = rest[3 * pp:]
    jj = pl.program_id(1)
    rows = FOX_HEADS * SUBLANES
    hd = FOX_HEAD_DIM
    row = lax.broadcasted_iota(I32, (rows, LANES), 0)
    lane = lax.broadcasted_iota(I32, (rows, LANES), 1)
    row_t = row % SUBLANES

    @pl.when(jj == 0)
    def _():
        q = q_ref[...]
        wrow = lax.broadcasted_iota(I32, (rows, FOX_WIDTH), 0)
        wcol = lax.broadcasted_iota(I32, (rows, FOX_WIDTH), 1)
        own = (wrow // SUBLANES) == (wcol // hd)
        qblk[...] = jnp.where(own, jnp.concatenate([q] * FOX_HEADS, axis=0), 0.0).astype(BF16)
        hmask[...] = jnp.where(own, 1.0, 0.0).astype(BF16)
        m_sc[...] = jnp.full(m_sc.shape, -jnp.inf, F32)
        l_sc[...] = jnp.zeros(l_sc.shape, F32)
        acc_sc[...] = jnp.zeros(acc_sc.shape, F32)
        carry_sc[...] = jnp.zeros(carry_sc.shape, F32)
        ui = lax.broadcasted_iota(I32, (LANES, LANES), 0)
        uj = lax.broadcasted_iota(I32, (LANES, LANES), 1)
        cnt = jnp.dot(lfnt_ref[0], (ui <= uj).astype(F32), precision=HIGHEST, preferred_element_type=F32)
        cnk = _expand_heads(cnt)
        cnk_sc[...] = cnk
        cq_sc[...] = jnp.sum(jnp.where(lane == row_t, cnk, 0.0), axis=1, keepdims=True)

    def attend(kmat, vstack, bias, mask):
        s = lax.dot_general(qblk[...], kmat, (((1,), (1,)), ((), ())), preferred_element_type=F32) + bias
        if mask is not None:
            s = jnp.where(mask, s, -jnp.inf)
        m = m_sc[...]
        m_new = jnp.maximum(m, jnp.max(s, axis=-1, keepdims=True))
        alpha = jnp.exp(m - m_new)
        p = jnp.exp(s - m_new)
        l_sc[...] = alpha * l_sc[...] + jnp.sum(p, axis=-1, keepdims=True)
        m_sc[...] = m_new
        pb = p.astype(BF16)
        hm = hmask[...]
        pexp = jnp.concatenate(
            [jnp.concatenate([pb[:, i * PAGE_SIZE:(i + 1) * PAGE_SIZE]] * FOX_HEADS, axis=1) * hm
             for i in range(s.shape[1] // PAGE_SIZE)], axis=1)
        acc_sc[...] = alpha * acc_sc[...] + jnp.dot(pexp, vstack, preferred_element_type=F32)

    ui = lax.broadcasted_iota(I32, (LANES, LANES), 0)
    uj = lax.broadcasted_iota(I32, (LANES, LANES), 1)
    later = (ui > uj).astype(F32)
    head_rows = lambda ref, h: ref[0, pl.ds(h, PAGE_SIZE, stride=FOX_HEADS), :].astype(BF16)
    carry = carry_sc[...]
    biases, kmats, vstacks = [], [], []
    for kc_ref, vc_ref, lft_ref in zip(kc_refs, vc_refs, lft_refs):
        lft = lft_ref[0]
        dt = jnp.dot(lft, later, precision=HIGHEST, preferred_element_type=F32) + carry
        carry = carry + jnp.sum(lft, axis=1, keepdims=True)
        biases.append(_expand_heads(dt))
        kmats.append(jnp.concatenate([head_rows(kc_ref, h) for h in range(FOX_HEADS)], axis=1))
        vstacks.append(jnp.concatenate([head_rows(vc_ref, h) for h in range(FOX_HEADS)], axis=0))
    carry_sc[...] = carry
    attend(jnp.concatenate(kmats, axis=0), jnp.concatenate(vstacks, axis=0),
           jnp.concatenate(biases, axis=1) + cq_sc[...], None)

    @pl.when(jj == n_steps - 1)
    def _():
        zrows = PAGE_SIZE - SUBLANES
        kmat = jnp.concatenate([kn_ref[...], jnp.zeros((zrows, FOX_WIDTH), F32)], axis=0).astype(BF16)
        vn = vn_ref[...]
        zpad = jnp.zeros((zrows, hd), F32)
        vstack = jnp.concatenate(
            [jnp.concatenate([vn[:, h * hd:(h + 1) * hd], zpad], axis=0) for h in range(FOX_HEADS)],
            axis=0).astype(BF16)
        bias = cq_sc[...] - cnk_sc[...]
        mask = (lane < SUBLANES) & (lane <= row_t)
        attend(kmat, vstack, bias, mask)
        out = acc_sc[...] / l_sc[...]
        for h in range(FOX_HEADS):
            o_ref[:, h * hd:(h + 1) * hd] = out[h * SUBLANES:(h + 1) * SUBLANES, :]


def fox_sample_attention(page_table, q, cache_k, cache_v, cache_logf_t, k_new, v_new, logf_new_t):
    bd, n_pages = page_table.shape
    hd = FOX_HEAD_DIM
    rows = FOX_HEADS * SUBLANES

    pp = FOX_PAGES_PER_STEP
    n_steps = n_pages // pp

    def page_map(i, trailing):
        def index(b, j, pt):
            return (pt[b * n_pages + (n_pages - 1 - (j * pp + i))],) + (0,) * trailing
        return index

    row_spec = pl.BlockSpec((SUBLANES, FOX_WIDTH), lambda b, j, pt: (b, 0))
    page_spec = lambda i: pl.BlockSpec((1, PAGE_SIZE * FOX_HEADS, hd), page_map(i, 2))
    pool = cache_k.shape[0]
    cache_k = cache_k.reshape(pool, PAGE_SIZE * FOX_HEADS, hd)
    cache_v = cache_v.reshape(pool, PAGE_SIZE * FOX_HEADS, hd)
    grid_spec = pltpu.PrefetchScalarGridSpec(
        num_scalar_prefetch=1,
        grid=(bd, n_steps),
        in_specs=(
            [row_spec]
            + [page_spec(i) for i in range(pp)]
            + [page_spec(i) for i in range(pp)]
            + [pl.BlockSpec((1, FOX_HEADS, PAGE_SIZE), page_map(i, 2)) for i in range(pp)]
            + [row_spec, row_spec, pl.BlockSpec((1, FOX_HEADS, LANES), lambda b, j, pt: (b, 0, 0))]
        ),
        out_specs=row_spec,
        scratch_shapes=[
            pltpu.VMEM((rows, FOX_WIDTH), BF16),
            pltpu.VMEM((rows, FOX_HEADS * PAGE_SIZE), BF16),
            pltpu.VMEM((rows, 1), F32),
            pltpu.VMEM((rows, 1), F32),
            pltpu.VMEM((rows, hd), F32),
            pltpu.VMEM((FOX_HEADS, 1), F32),
            pltpu.VMEM((rows, 1), F32),
            pltpu.VMEM((rows, LANES), F32),
        ],
    )
    return pl.pallas_call(
        functools.partial(_fox_sample_kernel, n_steps=n_steps),
        grid_spec=grid_spec,
        out_shape=jax.ShapeDtypeStruct((bd * SUBLANES, FOX_WIDTH), F32),
        compiler_params=_cparams("arbitrary", "arbitrary"),
        name="fox_sample_attention",
    )(page_table.reshape(-1), q, *([cache_k] * pp), *([cache_v] * pp), *([cache_logf_t] * pp),
      k_new, v_new, logf_new_t)


def _group_norm_gate(o, gain, gate):
    mu = jnp.mean(o, axis=-1, keepdims=True)
    d = o - mu
    var = jnp.mean(d * d, axis=-1, keepdims=True)
    y = d * lax.rsqrt(var + EPS) * gain
    return y * (gate * jax.nn.sigmoid(gate))


def _ret_prompt_kernel(q_ref, k_ref, v_ref, gr_ref, g_ref, dmat_ref, qdec_ref, kdec_ref, sdec_ref,
                       o_ref, st_ref, state):
    state[...] = jnp.zeros_like(state)
    n_chunks = q_ref.shape[0] // RET_CHUNK
    dmat = dmat_ref[0]
    qdec = qdec_ref[0]
    kdec = kdec_ref[0]
    sdec = sdec_ref[0]
    gain = g_ref[...]

    def body(c, _):
        r = pl.multiple_of(c * RET_CHUNK, RET_CHUNK)
        qc = q_ref[pl.ds(r, RET_CHUNK), :]
        kc = k_ref[pl.ds(r, RET_CHUNK), :]
        vc = v_ref[pl.ds(r, RET_CHUNK), :]
        s = lax.dot_general(qc, kc, (((1,), (1,)), ((), ())), preferred_element_type=F32) * dmat
        st = state[...]
        o = jnp.dot(s.astype(BF16), vc, preferred_element_type=F32)
        o = o + jnp.dot((qc.astype(F32) * qdec).astype(BF16), st.astype(BF16), preferred_element_type=F32)
        kd = (kc.astype(F32) * kdec).astype(BF16)
        state[...] = st * sdec + lax.dot_general(kd, vc, (((0,), (0,)), ((), ())), preferred_element_type=F32)
        o_ref[pl.ds(r, RET_CHUNK), :] = _group_norm_gate(o, gain, gr_ref[pl.ds(r, RET_CHUNK), :]).astype(o_ref.dtype)
        return 0

    lax.fori_loop(0, n_chunks, body, 0)
    st_ref[0, 0] = state[...]


def retention_prompt(qr, kr, vr, gates, ret_norm_g, consts, batch, seq):
    dmat, qdec, kdec, sdec = consts
    blk = lambda b, h: (b, h)
    hconst = lambda b, h: (h, 0, 0)
    return pl.pallas_call(
        _ret_prompt_kernel,
        grid=(batch, RET_HEADS),
        in_specs=[
            pl.BlockSpec((seq, RET_DK), blk),
            pl.BlockSpec((seq, RET_DK), blk),
            pl.BlockSpec((seq, RET_DV), blk),
            pl.BlockSpec((seq, RET_DV), blk),
            pl.BlockSpec((1, RET_DV), lambda b, h: (0, h)),
            pl.BlockSpec((1, RET_CHUNK, RET_CHUNK), hconst),
            pl.BlockSpec((1, RET_CHUNK, RET_DK), hconst),
            pl.BlockSpec((1, RET_CHUNK, RET_DK), hconst),
            pl.BlockSpec((1, 1, RET_DV), hconst),
        ],
        out_specs=[
            pl.BlockSpec((seq, RET_DV), blk),
            pl.BlockSpec((1, 1, RET_DK, RET_DV), lambda b, h: (b, h, 0, 0)),
        ],
        out_shape=[
            jax.ShapeDtypeStruct((batch * seq, RET_WIDTH), BF16),
            jax.ShapeDtypeStruct((batch, RET_HEADS, RET_DK, RET_DV), F32),
        ],
        scratch_shapes=[pltpu.VMEM((RET_DK, RET_DV), F32)],
        compiler_params=_cparams("arbitrary", "arbitrary"),
        name="retention_prompt",
    )(qr, kr, vr, gates, ret_norm_g, dmat, qdec, kdec, sdec)


RET_SAMPLE_Q_ROWS = 16


def _ret_sample_kernel(q_ref, k_ref, v_ref, gr_ref, g_ref, st_in, dmat_ref, qdec_ref, kdec_ref, sdec_ref,
                       o_ref, st_out):
    t = q_ref.shape[0]
    qpad = jnp.zeros((RET_SAMPLE_Q_ROWS - t, RET_DK), F32)
    kpad = jnp.zeros((RET_CHUNK - t, RET_DK), F32)
    for h in range(RET_HEADS):
        cols = slice(h * RET_DK, (h + 1) * RET_DK)
        q = jnp.concatenate([q_ref[:, cols], qpad], axis=0)
        k = jnp.concatenate([k_ref[:, cols], kpad], axis=0)
        v = jnp.concatenate([v_ref[:, cols], kpad], axis=0).astype(BF16)
        st = st_in[0, h]
        s = lax.dot_general(q.astype(BF16), k.astype(BF16), (((1,), (1,)), ((), ())),
                            preferred_element_type=F32) * dmat_ref[h]
        o = jnp.dot(s.astype(BF16), v, preferred_element_type=F32)
        o = o + jnp.dot((q * qdec_ref[h]).astype(BF16), st.astype(BF16), preferred_element_type=F32)
        kd = (k * kdec_ref[h]).astype(BF16)
        st_out[0, h] = st * sdec_ref[h] + lax.dot_general(kd, v, (((0,), (0,)), ((), ())),
                                                         preferred_element_type=F32)
        o_ref[:, cols] = _group_norm_gate(o[:t, :], g_ref[:, cols], gr_ref[:, cols])


def retention_sample(qr, kr, vr, gr, ret_norm_g, state, consts):
    dmat, qdec, kdec, sdec = consts
    bd = state.shape[0]
    t = qr.shape[0] // bd
    row = lambda b: (b, 0)
    const3 = lambda b: (0, 0, 0)
    return pl.pallas_call(
        _ret_sample_kernel,
        grid=(bd,),
        in_specs=[
            pl.BlockSpec((t, RET_WIDTH), row),
            pl.BlockSpec((t, RET_WIDTH), row),
            pl.BlockSpec((t, RET_WIDTH), row),
            pl.BlockSpec((t, RET_WIDTH), row),
            pl.BlockSpec((1, RET_WIDTH), lambda b: (0, 0)),
            pl.BlockSpec((1, RET_HEADS, RET_DK, RET_DV), lambda b: (b, 0, 0, 0)),
            pl.BlockSpec(dmat.shape, const3),
            pl.BlockSpec(qdec.shape, const3),
            pl.BlockSpec(kdec.shape, const3),
            pl.BlockSpec(sdec.shape, const3),
        ],
        out_specs=[
            pl.BlockSpec((t, RET_WIDTH), row),
            pl.BlockSpec((1, RET_HEADS, RET_DK, RET_DV), lambda b: (b, 0, 0, 0)),
        ],
        out_shape=[
            jax.ShapeDtypeStruct((bd * t, RET_WIDTH), F32),
            jax.ShapeDtypeStruct(state.shape, F32),
        ],
        compiler_params=_cparams("arbitrary"),
        name="retention_sample",
    )(qr, kr, vr, gr, ret_norm_g, state, dmat, qdec, kdec, sdec)


def _retention_consts(chunk, q_rows, k_rows):
    lg = jnp.log1p(-jnp.exp2(-5.0 - jnp.arange(RET_HEADS, dtype=F32)))
    n = jnp.arange(chunk, dtype=F32)
    diff = n[:, None] - n[None, :]
    decay = jnp.where(diff[None] >= 0, jnp.exp(jnp.maximum(diff, 0.0)[None] * lg[:, None, None]), 0.0)
    dmat = jnp.zeros((RET_HEADS, q_rows, k_rows), F32).at[:, :chunk, :chunk].set(decay)
    q_dec = jnp.exp((n + 1.0)[None, :] * lg[:, None])
    k_dec = jnp.exp((chunk - 1.0 - n)[None, :] * lg[:, None])
    qdec = jnp.zeros((RET_HEADS, q_rows, RET_DK), F32).at[:, :chunk, :].set(
        jnp.broadcast_to(q_dec[:, :, None], (RET_HEADS, chunk, RET_DK)))
    kdec = jnp.zeros((RET_HEADS, k_rows, RET_DK), F32).at[:, :chunk, :].set(
        jnp.broadcast_to(k_dec[:, :, None], (RET_HEADS, chunk, RET_DK)))
    sdec = jnp.broadcast_to(jnp.exp(chunk * lg)[:, None, None], (RET_HEADS, 1, RET_DV))
    return dmat, qdec, kdec, sdec


ROUTER_LANES = LANES


def _resid_router_kernel(x_ref, mx_ref, g1_ref, sh_ref, sc_ref, g_ref, wr_ref, br_ref,
                         x1_ref, h2_ref, tw_ref, ti_ref):
    x1 = x_ref[...] + g1_ref[...] * mx_ref[...]
    x1_ref[...] = x1
    ms = jnp.mean(x1 * x1, axis=-1, keepdims=True)
    y = x1 * lax.rsqrt(ms + EPS) * g_ref[...]
    h3 = y * (1.0 + sc_ref[...]) + sh_ref[...]
    rows = h3.shape[0] * h3.shape[1]
    h2 = h3.reshape(rows, h3.shape[2])

    half = h2.shape[1] // 2
    bits = pltpu.bitcast(h2.astype(BF16).astype(F32), jnp.uint32)
    h2_ref[...] = (bits[:, :half] >> 16) | (bits[:, half:] & jnp.uint32(0xFFFF0000))

    logits = jnp.dot(h2, wr_ref[...], precision=HIGHEST, preferred_element_type=F32) + br_ref[...]
    lane = lax.broadcasted_iota(I32, logits.shape, 1)
    logits = jnp.where(lane < N_EXPERTS, logits, -jnp.inf)
    vals, idxs = [], []
    for _ in range(TOP_K):
        m = jnp.max(logits, axis=-1, keepdims=True)
        idx = jnp.min(jnp.where(logits == m, lane, ROUTER_LANES), axis=-1, keepdims=True)
        vals.append(m)
        idxs.append(idx)
        logits = jnp.where(lane == idx, -jnp.inf, logits)
    exps = [jnp.exp(v - vals[0]) for v in vals]
    denom = exps[0] + exps[1] + exps[2] + exps[3]
    tw = jnp.zeros(lane.shape, F32)
    ti = jnp.zeros(lane.shape, I32)
    for k in range(TOP_K):
        tw = jnp.where(lane == k, exps[k] / denom, tw)
        ti = jnp.where(lane == k, idxs[k], ti)
    tw_ref[...] = tw
    ti_ref[...] = ti


def resid_norm_router(x3, mixed3, mod4, gate_chunk, shift_chunk, scale_chunk, gain, w_router_pad, b_router_pad,
                      prompt_groups, groups_per_seq):
    g, r, d = x3.shape
    gb = NORM_GROUPS
    rows = gb * r
    n = g * r
    grp = lambda i: (i, 0, 0)
    tok = lambda i: (i, 0)
    mod_spec = functools.partial(_mod_spec, gb=gb, prompt_groups=prompt_groups, groups_per_seq=groups_per_seq, d=d)
    return pl.pallas_call(
        _resid_router_kernel,
        grid=(g // gb,),
        in_specs=[
            pl.BlockSpec((gb, r, d), grp),
            pl.BlockSpec((gb, r, d), grp),
            mod_spec(gate_chunk),
            mod_spec(shift_chunk),
            mod_spec(scale_chunk),
            pl.BlockSpec((1, 1, d), lambda i: (0, 0, 0)),
            pl.BlockSpec((d, ROUTER_LANES), lambda i: (0, 0)),
            pl.BlockSpec((1, ROUTER_LANES), lambda i: (0, 0)),
        ],
        out_specs=[
            pl.BlockSpec((gb, r, d), grp),
            pl.BlockSpec((rows, d // 2), tok),
            pl.BlockSpec((rows, ROUTER_LANES), tok),
            pl.BlockSpec((rows, ROUTER_LANES), tok),
        ],
        out_shape=[
            jax.ShapeDtypeStruct((g, r, d), F32),
            jax.ShapeDtypeStruct((n, d // 2), jnp.uint32),
            jax.ShapeDtypeStruct((n, ROUTER_LANES), F32),
            jax.ShapeDtypeStruct((n, ROUTER_LANES), I32),
        ],
        compiler_params=_cparams("arbitrary"),
        name="resid_norm_router",
    )(x3, mixed3, mod4, mod4, mod4, gain.reshape(1, 1, d), w_router_pad, b_router_pad)


MOE_SUB = 256
MOE_SUBS_PER_TILE = 5
MOE_TILE = MOE_SUB * MOE_SUBS_PER_TILE
FF_TN = 256
DOWN_TN = 512
COMBINE_TOKENS = 64


def _dispatch_kernel(tok_ref, start_ref, nsub_ref, ob_ref, src_hbm, o_ref, sem):
    s = pl.program_id(0)
    i = pl.program_id(1)
    base = start_ref[s] + i * MOE_SUB
    last = tok_ref.shape[0] - 1

    @pl.when(i < nsub_ref[s])
    def _():
        def issue(r, _):
            t = tok_ref[jnp.minimum(base + r, last)]
            pltpu.make_async_copy(src_hbm.at[pl.ds(t, 1), :], o_ref.at[pl.ds(r, 1), :], sem).start()
            return 0

        lax.fori_loop(0, MOE_SUB, issue, 0, unroll=8)
        pltpu.make_async_copy(src_hbm.at[pl.ds(0, MOE_SUB), :], o_ref, sem).wait()


def moe_dispatch(src_tok, tile_start, st_nsub, st_out_block, h2_packed, m_pad):
    width = h2_packed.shape[1]
    grid_spec = pltpu.PrefetchScalarGridSpec(
        num_scalar_prefetch=4,
        grid=(st_nsub.shape[0], MOE_SUBS_PER_TILE),
        in_specs=[pl.BlockSpec(memory_space=pl.ANY)],
        out_specs=pl.BlockSpec((MOE_SUB, width),
                               lambda s, i, tok, start, ns, ob: (ob[s * MOE_SUBS_PER_TILE + i], 0)),
        scratch_shapes=[pltpu.SemaphoreType.DMA(())],
    )
    return pl.pallas_call(
        _dispatch_kernel,
        grid_spec=grid_spec,
        out_shape=jax.ShapeDtypeStruct((m_pad, width), h2_packed.dtype),
        compiler_params=_cparams("arbitrary", "arbitrary"),
        name="moe_dispatch",
    )(src_tok, tile_start, st_nsub, st_out_block, h2_packed)


def _unpack_bf16_pairs(words):
    lo = pltpu.bitcast(words << 16, F32).astype(BF16)
    hi = pltpu.bitcast(words & jnp.uint32(0xFFFF0000), F32).astype(BF16)
    return lo, hi


def _moe_up_kernel(e_ref, blk_ref, ns_ref, x_ref, wg_ref, wu_ref, bg_ref, bu_ref, o_ref,
                   xlo, xhi, wg_bf, wu_bf):
    s = pl.program_id(0)
    nsub = ns_ref[s]
    half = xlo.shape[1]

    @pl.when((pl.program_id(1) == 0) & (nsub > 0))
    def _():
        for i in range(MOE_SUBS_PER_TILE):
            @pl.when(i < nsub)
            def _():
                rows = slice(i * MOE_SUB, (i + 1) * MOE_SUB)
                lo, hi = _unpack_bf16_pairs(x_ref[rows, :])
                xlo[rows, :] = lo
                xhi[rows, :] = hi

    def compute(i):
        rows = slice(i * MOE_SUB, (i + 1) * MOE_SUB)
        lo = xlo[rows, :]
        hi = xhi[rows, :]
        g = (jnp.dot(lo, wg_bf[:half, :], preferred_element_type=F32)
             + jnp.dot(hi, wg_bf[half:, :], preferred_element_type=F32) + bg_ref[...])
        u = (jnp.dot(lo, wu_bf[:half, :], preferred_element_type=F32)
             + jnp.dot(hi, wu_bf[half:, :], preferred_element_type=F32) + bu_ref[...])
        g = jnp.minimum(g, SWIGLU_LIMIT)
        u = jnp.clip(u, -SWIGLU_LIMIT, SWIGLU_LIMIT)
        o_ref[rows, :] = ((u + 1.0) * g * jax.nn.sigmoid(SWIGLU_ALPHA * g)).astype(o_ref.dtype)

    @pl.when(nsub == MOE_SUBS_PER_TILE)
    def _():
        _cast_rows(wg_ref, wg_bf)
        _cast_rows(wu_ref, wu_bf)
        for i in range(MOE_SUBS_PER_TILE):
            compute(i)

    @pl.when((nsub > 0) & (nsub < MOE_SUBS_PER_TILE))
    def _():
        _cast_rows(wg_ref, wg_bf)
        _cast_rows(wu_ref, wu_bf)
        for i in range(MOE_SUBS_PER_TILE):
            @pl.when(i < nsub)
            def _():
                compute(i)

            @pl.when(i >= nsub)
            def _():
                rows = slice(i * MOE_SUB, (i + 1) * MOE_SUB)
                o_ref[rows, :] = jnp.zeros((MOE_SUB, o_ref.shape[1]), o_ref.dtype)


def _frozen_col(n, ns, s, last):
    return jnp.where(ns[s] > 0, n, last)


def moe_up(st_expert, st_block, st_nsub, xs, w_gate_up, b_gate_up):
    m_pad, half = xs.shape
    d = 2 * half
    n_steps = st_expert.shape[0]
    nb = D_FF // FF_TN
    last = nb - 1
    grid_spec = pltpu.PrefetchScalarGridSpec(
        num_scalar_prefetch=3,
        grid=(n_steps, nb),
        in_specs=[
            pl.BlockSpec((MOE_TILE, half), lambda s, n, e, blk, ns: (blk[s], 0)),
            pl.BlockSpec((None, d, FF_TN), lambda s, n, e, blk, ns: (e[s], 0, _frozen_col(n, ns, s, last))),
            pl.BlockSpec((None, d, FF_TN), lambda s, n, e, blk, ns: (e[s], 0, nb + _frozen_col(n, ns, s, last))),
            pl.BlockSpec((None, 1, FF_TN), lambda s, n, e, blk, ns: (e[s], 0, _frozen_col(n, ns, s, last))),
            pl.BlockSpec((None, 1, FF_TN), lambda s, n, e, blk, ns: (e[s], 0, nb + _frozen_col(n, ns, s, last))),
        ],
        out_specs=pl.BlockSpec((MOE_TILE, FF_TN), lambda s, n, e, blk, ns: (blk[s], _frozen_col(n, ns, s, last))),
        scratch_shapes=[
            pltpu.VMEM((MOE_TILE, half), BF16),
            pltpu.VMEM((MOE_TILE, half), BF16),
            pltpu.VMEM((d, FF_TN), BF16),
            pltpu.VMEM((d, FF_TN), BF16),
        ],
    )
    return pl.pallas_call(
        _moe_up_kernel,
        grid_spec=grid_spec,
        out_shape=jax.ShapeDtypeStruct((m_pad, D_FF), BF16),
        compiler_params=_cparams("arbitrary", "arbitrary"),
        name="moe_up",
    )(st_expert, st_block, st_nsub, xs, w_gate_up, w_gate_up, b_gate_up, b_gate_up)


def _moe_down_kernel(e_ref, blk_ref, ns_ref, x_ref, w_ref, b_ref, o_ref, w_bf):
    nsub = ns_ref[pl.program_id(0)]

    def compute(i):
        rows = slice(i * MOE_SUB, (i + 1) * MOE_SUB)
        o_ref[rows, :] = jnp.dot(x_ref[rows, :], w_bf[...], preferred_element_type=F32) + b_ref[...]

    @pl.when(nsub == MOE_SUBS_PER_TILE)
    def _():
        _cast_rows(w_ref, w_bf)
        for i in range(MOE_SUBS_PER_TILE):
            compute(i)

    @pl.when((nsub > 0) & (nsub < MOE_SUBS_PER_TILE))
    def _():
        _cast_rows(w_ref, w_bf)
        for i in range(MOE_SUBS_PER_TILE):
            @pl.when(i < nsub)
            def _():
                compute(i)

            @pl.when(i >= nsub)
            def _():
                rows = slice(i * MOE_SUB, (i + 1) * MOE_SUB)
                o_ref[rows, :] = jnp.zeros((MOE_SUB, o_ref.shape[1]), o_ref.dtype)


def moe_down(st_expert, st_block, st_nsub, act, w_down, b_down):
    m_pad, dff = act.shape
    d = w_down.shape[2]
    n_steps = st_expert.shape[0]
    nb = d // DOWN_TN
    last = nb - 1
    grid_spec = pltpu.PrefetchScalarGridSpec(
        num_scalar_prefetch=3,
        grid=(n_steps, nb),
        in_specs=[
            pl.BlockSpec((MOE_TILE, dff), lambda s, n, e, blk, ns: (blk[s], 0)),
            pl.BlockSpec((None, dff, DOWN_TN), lambda s, n, e, blk, ns: (e[s], 0, _frozen_col(n, ns, s, last))),
            pl.BlockSpec((None, 1, DOWN_TN), lambda s, n, e, blk, ns: (e[s], 0, _frozen_col(n, ns, s, last))),
        ],
        out_specs=pl.BlockSpec((MOE_TILE, DOWN_TN), lambda s, n, e, blk, ns: (blk[s], _frozen_col(n, ns, s, last))),
        scratch_shapes=[pltpu.VMEM((dff, DOWN_TN), BF16)],
    )
    return pl.pallas_call(
        _moe_down_kernel,
        grid_spec=grid_spec,
        out_shape=jax.ShapeDtypeStruct((m_pad, d), F32),
        compiler_params=_cparams("arbitrary", "arbitrary"),
        name="moe_down",
    )(st_expert, st_block, st_nsub, act, w_down, b_down)


def _combine_kernel(pos_ref, ys_hbm, x1_ref, g2_ref, tw_ref, o_ref, buf, sems):
    i = pl.program_id(0)
    n_steps = pl.num_programs(0)
    t = COMBINE_TOKENS

    def row_copy(slot, k, r, p):
        return pltpu.make_async_copy(ys_hbm.at[pl.ds(p, 1), :], buf.at[slot, k, pl.ds(r, 1), :], sems.at[slot])

    def start(step, slot):
        def issue(r, _):
            for k in range(TOP_K):
                row_copy(slot, k, r, pos_ref[(step * t + r) * TOP_K + k]).start()
            return 0

        lax.fori_loop(0, t, issue, 0, unroll=4)

    @pl.when(i == 0)
    def _():
        start(0, 0)

    @pl.when(i + 1 < n_steps)
    def _():
        start(i + 1, (i + 1) % 2)

    slot = i % 2
    for k in range(TOP_K):
        pltpu.make_async_copy(ys_hbm.at[pl.ds(0, t), :], buf.at[slot, k], sems.at[slot]).wait()
    tw = tw_ref[...]
    y = tw[:, 0:1] * buf[slot, 0]
    for k in range(1, TOP_K):
        y = y + tw[:, k:k + 1] * buf[slot, k]
    d = y.shape[1]
    y3 = y.reshape(t // SUBLANES, SUBLANES, d)
    o_ref[...] = x1_ref[...] + g2_ref[...] * y3


def moe_combine(pos, ys, x1_3, mod4, gate_chunk, top_w, prompt_groups, groups_per_seq):
    g, r, d = x1_3.shape
    gb = COMBINE_TOKENS // r
    grid_spec = pltpu.PrefetchScalarGridSpec(
        num_scalar_prefetch=1,
        grid=(g // gb,),
        in_specs=[
            pl.BlockSpec(memory_space=pl.ANY),
            pl.BlockSpec((gb, r, d), lambda i, pos: (i, 0, 0)),
            _mod_spec(gate_chunk, gb, prompt_groups, groups_per_seq, d),
            pl.BlockSpec((COMBINE_TOKENS, ROUTER_LANES), lambda i, pos: (i, 0)),
        ],
        out_specs=pl.BlockSpec((gb, r, d), lambda i, pos: (i, 0, 0)),
        scratch_shapes=[
            pltpu.VMEM((2, TOP_K, COMBINE_TOKENS, d), F32),
            pltpu.SemaphoreType.DMA((2,)),
        ],
    )
    return pl.pallas_call(
        _combine_kernel,
        grid_spec=grid_spec,
        out_shape=jax.ShapeDtypeStruct((g, r, d), F32),
        compiler_params=_cparams("arbitrary"),
        name="moe_combine",
    )(pos, ys, x1_3, mod4, top_w)


def _routing_tables(top_i, n_tiles):
    n = top_i.shape[0]
    m = n * TOP_K
    flat_e = top_i.reshape(m)
    order = jnp.argsort(flat_e, stable=True).astype(I32)
    e_sorted = flat_e[order]
    counts = jnp.sum(jax.nn.one_hot(flat_e, N_EXPERTS, dtype=I32), axis=0)
    starts = jnp.cumsum(counts) - counts
    tiles_per_e = (counts + MOE_TILE - 1) // MOE_TILE
    tile_end = jnp.cumsum(tiles_per_e)
    tile_start = tile_end - tiles_per_e
    total_tiles = tile_end[-1]
    rank = jnp.arange(m, dtype=I32) - starts[e_sorted]
    dest = tile_start[e_sorted] * MOE_TILE + rank
    pos = jnp.zeros((m,), I32).at[order].set(dest)
    src_tok = order // TOP_K
    s = jnp.arange(n_tiles, dtype=I32)
    s_eff = jnp.minimum(s, total_tiles - 1)
    st_expert = jnp.searchsorted(tile_end, s_eff, side="right").astype(I32)
    within = (s_eff - tile_start[st_expert]) * MOE_TILE
    rows_valid = jnp.where(s < total_tiles, jnp.clip(counts[st_expert] - within, 0, MOE_TILE), 0)
    st_nsub = ((rows_valid + MOE_SUB - 1) // MOE_SUB).astype(I32)
    sorted_start = (starts[st_expert] + within).astype(I32)
    sub = jnp.arange(MOE_SUBS_PER_TILE, dtype=I32)[None, :]
    own = jnp.where(sub < st_nsub[:, None], s[:, None] * MOE_SUBS_PER_TILE + sub, -1).reshape(-1)
    st_out_block = lax.cummax(own, axis=0).astype(I32)
    return pos, src_tok.astype(I32), sorted_start, st_expert, s_eff.astype(I32), st_nsub, st_out_block


def _rope_tables(pos):
    half = RET_DK // 2
    inv = ROPE_BASE ** (-jnp.arange(half, dtype=F32) / half)
    ang = pos.astype(F32)[:, None] * inv[None, :]
    return jnp.cos(ang), jnp.sin(ang)


def kernel(x_prompt, x_sample, c_prompt, c_sample, cache_k, cache_v, cache_logf, state_ret, page_table, rms1_g, rms2_g, w_ada, b_ada, w_in, b_forget, q_norm_g, k_norm_g, ret_norm_g, w_branch_fox, w_branch_ret, w_out, w_router, b_router, w_gate_up, b_gate_up, w_down, b_down):
    batch, seq, d = x_prompt.shape
    bd, dec_t, _ = x_sample.shape
    depth = w_in.shape[0]
    assert depth == 1 and d == D_MODEL and dec_t == SUBLANES
    n_p = batch * seq
    n_s = bd * dec_t
    n = n_p + n_s
    groups = n // SUBLANES
    past_len = page_table.shape[1] * cache_k.shape[2]

    c_all = jnp.concatenate([jnp.repeat(c_prompt, MOD_REPEAT, axis=0), c_sample], axis=0)
    mod = ada_modulation(c_all, w_ada[0], b_ada[0].reshape(1, -1))
    mod4 = jnp.transpose(mod.reshape(c_all.shape[0], N_MOD, 1, d), (1, 0, 2, 3))
    shift1, scale1, gate1, shift2, scale2, gate2 = range(N_MOD)
    prompt_groups = n_p // SUBLANES
    groups_per_seq = seq // SUBLANES

    x_all = jnp.concatenate([x_prompt.reshape(n_p, d), x_sample.reshape(n_s, d)], axis=0)
    x3 = x_all.reshape(groups, SUBLANES, d)
    h = norm_modulate(x3, mod4, shift1, scale1, rms1_g[0], prompt_groups, groups_per_seq)

    pos = jnp.concatenate([jnp.tile(jnp.arange(seq, dtype=I32), batch),
                           jnp.tile(past_len + jnp.arange(dec_t, dtype=I32), bd)])
    cos, sin = _rope_tables(pos)
    w_in_t = jnp.swapaxes(w_in[0], 0, 1)
    b_forget_pad = jnp.zeros((1, LANES), F32).at[0, :FOX_HEADS].set(b_forget[0])
    q, k32, kbf, v32, vbf, logf, qr, kr, vr, gates = in_projection(
        h, w_in_t, b_forget_pad, q_norm_g[0].reshape(1, -1), k_norm_g[0].reshape(1, -1), cos, sin)

    c_cum, c_cum_t = cumsum_log_forget(logf, batch, seq)
    o_fox_p = fox_prompt_attention(q, kbf, vbf, c_cum, c_cum_t, batch, seq)
    cache_logf_t = jnp.swapaxes(cache_logf[0], 1, 2)
    logf_s = logf[n_p:, :FOX_HEADS].reshape(bd, dec_t, FOX_HEADS)
    logf_new_t = jnp.zeros((bd, FOX_HEADS, LANES), F32).at[:, :, :dec_t].set(jnp.swapaxes(logf_s, 1, 2))
    o_fox_s = fox_sample_attention(page_table, q[n_p:].astype(F32), cache_k[0], cache_v[0], cache_logf_t,
                                   k32[n_p:], v32[n_p:], logf_new_t)
    o_fox = jnp.concatenate([o_fox_p, o_fox_s.astype(BF16)], axis=0)

    g_ret = ret_norm_g[0].reshape(1, -1)
    o_ret_p, state_p = retention_prompt(qr, kr, vr, gates, g_ret,
                                        _retention_consts(RET_CHUNK, RET_CHUNK, RET_CHUNK), batch, seq)
    o_ret_s, state_s = retention_sample(qr[n_p:].astype(F32), kr[n_p:].astype(F32), vr[n_p:].astype(F32),
                                        gates[n_p:, :RET_WIDTH], g_ret, state_ret[0],
                                        _retention_consts(dec_t, RET_SAMPLE_Q_ROWS, RET_CHUNK))
    o_ret = jnp.concatenate([o_ret_p, o_ret_s.astype(BF16)], axis=0)

    gate_blocks = RET_WIDTH // TN
    (mixed_in,) = matmul_ws(
        [o_fox, o_ret], [w_branch_fox[0], w_branch_ret[0]], col_offsets=[0, 0], n_blocks=d // TN,
        epilogue=_ep_branch_merge, extra=(gates, gates),
        extra_specs=(_tile_spec(off=gate_blocks), _tile_spec(off=gate_blocks + d // TN)),
        out_shapes=[jax.ShapeDtypeStruct((n, d), BF16)], out_specs=[_tile_spec()], name="branch_merge")
    (mixed,) = matmul_ws([mixed_in], [w_out[0]], col_offsets=[0], n_blocks=d // TN, epilogue=_ep_plain,
                         out_shapes=[jax.ShapeDtypeStruct((n, d), F32)], out_specs=[_tile_spec()], name="out_proj")

    w_router_pad = jnp.zeros((d, ROUTER_LANES), F32).at[:, :N_EXPERTS].set(w_router[0])
    b_router_pad = jnp.zeros((1, ROUTER_LANES), F32).at[0, :N_EXPERTS].set(b_router[0])
    x1, h2_packed, top_w, top_i = resid_norm_router(
        x3, mixed.reshape(groups, SUBLANES, d), mod4, gate1, shift2, scale2, rms2_g[0], w_router_pad, b_router_pad,
        prompt_groups, groups_per_seq)

    n_tiles = N_EXPERTS + (n * TOP_K) // MOE_TILE
    pos_rows, src_tok, sorted_start, st_expert, st_block, st_nsub, st_out_block = _routing_tables(
        top_i[:, :TOP_K], n_tiles)
    xs = moe_dispatch(src_tok, sorted_start, st_nsub, st_out_block, h2_packed, n_tiles * MOE_TILE)
    act = moe_up(st_expert, st_block, st_nsub, xs, w_gate_up[0], b_gate_up[0].reshape(N_EXPERTS, 1, -1))
    ys = moe_down(st_expert, st_block, st_nsub, act, w_down[0], b_down[0].reshape(N_EXPERTS, 1, -1))
    y = moe_combine(pos_rows, ys, x1, mod4, gate2, top_w, prompt_groups, groups_per_seq).reshape(n, d)

    y_prompt = y[:n_p].reshape(batch, seq, d)
    y_sample = y[n_p:].reshape(bd, dec_t, d)
    kv_p = (depth, batch, seq, FOX_HEADS, FOX_HEAD_DIM)
    kv_s = (depth, bd, dec_t, FOX_HEADS, FOX_HEAD_DIM)
    logf16 = logf[:, :FOX_HEADS]
    return (y_prompt, y_sample,
            k32[:n_p].reshape(kv_p), v32[:n_p].reshape(kv_p), logf16[:n_p].reshape(depth, batch, seq, FOX_HEADS),
            state_p[None],
            k32[n_p:].reshape(kv_s), v32[n_p:].reshape(kv_s), logf16[n_p:].reshape(depth, bd, dec_t, FOX_HEADS),
            state_s[None])
```
